```python
import math
import jax, jax.numpy as jnp
from jax import lax
import numpy as np

D_MODEL = 4096
BATCH = 4
SEQ = 2048
DEPTH = 1
DEC_BATCH = 16
DEC_SEQ = 64
PAST_LEN = 1024

CHUNK = 64
Q_BLOCK = 128
MIX_W = D_MODEL // 2
H_FOX = 16
DH_FOX = MIX_W // H_FOX
H_DIFF = 16
DH_DIFF = MIX_W // (2 * H_DIFF)
ROPE_DIM = DH_DIFF // 4
ROPE_THETA = 500000.0
FORGET_BIAS_INIT = 3.0
N_EXPERTS = 32
TOP_K = 4
D_EXPERT = D_MODEL
SWIGLU_LIMIT = 7.0
SWIGLU_ALPHA = 1.702
MOE_BLOCK = 256
NORM_EPS = 1e-5

OFF_FOX_Q = 0
OFF_FOX_K = OFF_FOX_Q + MIX_W
OFF_FOX_V = OFF_FOX_K + MIX_W
OFF_FOX_F = OFF_FOX_V + MIX_W
OFF_DIFF_Q = OFF_FOX_F + H_FOX
OFF_DIFF_K = OFF_DIFF_Q + MIX_W
OFF_DIFF_V = OFF_DIFF_K + MIX_W
OFF_GATE = OFF_DIFF_V + MIX_W
N_IN = OFF_GATE + 2 * D_MODEL

kernel_name = "hybrid_fox_diff_moe_stream_step"


def rms_norm(x, g):
    xf = x.astype(jnp.float32)
    y = xf * lax.rsqrt(jnp.mean(xf * xf, axis=-1, keepdims=True) + NORM_EPS)
    return (y * g.astype(jnp.float32)).astype(x.dtype)


def lambda_init(layer_idx):
    return 0.8 - 0.6 * math.exp(-0.3 * layer_idx)


def partial_rope(x, pos):
    half = ROPE_DIM // 2
    inv_freq = ROPE_THETA ** (-jnp.arange(half, dtype=jnp.float32) * (2.0 / ROPE_DIM))
    ang = pos.astype(jnp.float32)[:, None] * inv_freq[None, :]
    shape = (1, pos.shape[0]) + (1,) * (x.ndim - 3) + (half,)
    cos = jnp.cos(ang).reshape(shape).astype(x.dtype)
    sin = jnp.sin(ang).reshape(shape).astype(x.dtype)
    x1 = x[..., :half]
    x2 = x[..., half:ROPE_DIM]
    return jnp.concatenate([x1 * cos - x2 * sin, x2 * cos + x1 * sin, x[..., ROPE_DIM:]], axis=-1)


def sweep_query_blocks(block_fn, qpos, *q_args):
    sq = qpos.shape[0]
    if sq <= Q_BLOCK:
        return block_fn(qpos, *q_args)
    nb = sq // Q_BLOCK
    blocked = tuple(a.reshape((a.shape[0], nb, Q_BLOCK) + a.shape[2:]).swapaxes(0, 1) for a in q_args)
    out = lax.map(lambda t: block_fn(*t), (qpos.reshape(nb, Q_BLOCK),) + blocked)
    out = out.swapaxes(0, 1)
    return out.reshape((out.shape[0], sq) + out.shape[3:])


def forgetting_attention(q, cum_q, qpos, k, v, cum_k, kpos):
    scale = DH_FOX ** -0.5
    ck = jnp.swapaxes(cum_k, 1, 2)[:, :, None, :]

    def block(qp, qb, cqb):
        s = jnp.einsum('bqhd,bkhd->bhqk', qb, k, preferred_element_type=jnp.float32) * scale
        s = s + jnp.swapaxes(cqb, 1, 2)[..., None] - ck
        s = jnp.where(kpos[None, :] <= qp[:, None], s, -jnp.inf)
        p = jax.nn.softmax(s, axis=-1)
        return jnp.einsum('bhqk,bkhd->bqhd', p.astype(v.dtype), v)

    return sweep_query_blocks(block, qpos, q, cum_q)


def differential_attention(q, qpos, k, v, kpos, lam):
    scale = DH_DIFF ** -0.5

    def block(qp, qb):
        s = jnp.einsum('bqhcd,bkhcd->bhcqk', qb, k, preferred_element_type=jnp.float32) * scale
        visible = kpos[None, :] < (qp[:, None] // CHUNK + 1) * CHUNK
        s = jnp.where(visible, s, -jnp.inf)
        p = jax.nn.softmax(s, axis=-1)
        a = p[:, :, 0] - lam * p[:, :, 1]
        return jnp.einsum('bhqk,bkhe->bqhe', a.astype(v.dtype), v)

    return sweep_query_blocks(block, qpos, q)


def routed_moe(x, w_router, b_router, w_gate, b_gate, w_up, b_up, w_down, b_down):
    t = x.shape[0]
    logits = jnp.dot(x, w_router, preferred_element_type=jnp.float32) + b_router.astype(jnp.float32)
    top_v, top_i = lax.top_k(logits, TOP_K)
    gate_w = jax.nn.softmax(top_v, axis=-1)
    flat_e = top_i.reshape(-1).astype(jnp.int32)
    flat_t = jnp.repeat(jnp.arange(t, dtype=jnp.int32), TOP_K)
    flat_w = gate_w.reshape(-1)
    order = jnp.argsort(flat_e)
    sorted_e = flat_e[order]
    counts = jnp.zeros((N_EXPERTS,), jnp.int32).at[flat_e].add(1)
    starts = jnp.cumsum(counts) - counts
    padded = (counts + MOE_BLOCK - 1) // MOE_BLOCK * MOE_BLOCK
    padded_end = jnp.cumsum(padded)
    padded_start = padded_end - padded
    rank = jnp.arange(t * TOP_K, dtype=jnp.int32) - starts[sorted_e]
    dest = padded_start[sorted_e] + rank
    n_blocks = -(-(t * TOP_K) // MOE_BLOCK) + N_EXPERTS
    n_rows = n_blocks * MOE_BLOCK
    row_tok = jnp.zeros((n_rows,), jnp.int32).at[dest].set(flat_t[order])
    row_w = jnp.zeros((n_rows,), jnp.float32).at[dest].set(flat_w[order])
    block_e = jnp.minimum(
        jnp.searchsorted(padded_end, jnp.arange(n_blocks, dtype=jnp.int32) * MOE_BLOCK, side='right'),
        N_EXPERTS - 1)

    def expert_block(args):
        tok, w, e = args
        xb = x[tok]
        g = jnp.minimum(xb @ w_gate[e] + b_gate[e], SWIGLU_LIMIT)
        u = jnp.clip(xb @ w_up[e] + b_up[e], -SWIGLU_LIMIT, SWIGLU_LIMIT)
        h = (u + 1.0) * (g * jax.nn.sigmoid(SWIGLU_ALPHA * g))
        return (h @ w_down[e] + b_down[e]) * w.astype(x.dtype)[:, None]

    out = lax.map(expert_block, (row_tok.reshape(n_blocks, MOE_BLOCK),
                                 row_w.reshape(n_blocks, MOE_BLOCK), block_e))
    return jnp.zeros_like(x).at[row_tok].add(out.reshape(n_rows, x.shape[1]))


def hybrid_layer(x, qpos, past, layer_idx, norm_mix, w_in, b_forget, lambda_q1, lambda_k1,
                 lambda_q2, lambda_k2, diff_subln, w_branch, w_out, norm_ffn, w_router, b_router,
                 moe_w_gate, moe_b_gate, moe_w_up, moe_b_up, moe_w_down, moe_b_down):
    b, s, _ = x.shape
    xn = rms_norm(x, norm_mix)
    z = jnp.einsum('bsd,dn->bsn', xn, w_in)
    fq = z[..., OFF_FOX_Q:OFF_FOX_K].reshape(b, s, H_FOX, DH_FOX)
    fk = z[..., OFF_FOX_K:OFF_FOX_V].reshape(b, s, H_FOX, DH_FOX)
    fv = z[..., OFF_FOX_V:OFF_FOX_F].reshape(b, s, H_FOX, DH_FOX)
    logf = jax.nn.log_sigmoid(z[..., OFF_FOX_F:OFF_DIFF_Q].astype(jnp.float32)
                              + b_forget.astype(jnp.float32))
    dq = partial_rope(z[..., OFF_DIFF_Q:OFF_DIFF_K].reshape(b, s, H_DIFF, 2, DH_DIFF), qpos)
    dk = partial_rope(z[..., OFF_DIFF_K:OFF_DIFF_V].reshape(b, s, H_DIFF, 2, DH_DIFF), qpos)
    dv = z[..., OFF_DIFF_V:OFF_GATE].reshape(b, s, H_DIFF, 2 * DH_DIFF)
    gates = jax.nn.sigmoid(z[..., OFF_GATE:])
    gate_fox = gates[..., :D_MODEL]
    gate_diff = gates[..., D_MODEL:]
    new_state = (fk, fv, logf, dk.reshape(b, s, H_DIFF, 2 * DH_DIFF), dv)

    if past is None:
        fk_all, fv_all, logf_all, dk_all, dv_all = fk, fv, logf, dk, dv
        kpos = qpos
    else:
        pfk, pfv, plogf, pdk, pdv = past
        fk_all = jnp.concatenate([pfk, fk], axis=1)
        fv_all = jnp.concatenate([pfv, fv], axis=1)
        logf_all = jnp.concatenate([plogf.astype(jnp.float32), logf], axis=1)
        dk_all = jnp.concatenate([pdk.reshape(pdk.shape[0], pdk.shape[1], H_DIFF, 2, DH_DIFF), dk], axis=1)
        dv_all = jnp.concatenate([pdv, dv], axis=1)
        kpos = jnp.arange(fk_all.shape[1], dtype=jnp.int32)

    cum = jnp.cumsum(logf_all, axis=1)
    cum_q = cum[:, cum.shape[1] - s:]
    o_fox = forgetting_attention(fq, cum_q, qpos, fk_all, fv_all, cum, kpos).reshape(b, s, MIX_W)

    lam_init = lambda_init(layer_idx)
    f32 = jnp.float32
    lam = (jnp.exp(jnp.sum(lambda_q1.astype(f32) * lambda_k1.astype(f32)))
           - jnp.exp(jnp.sum(lambda_q2.astype(f32) * lambda_k2.astype(f32))) + lam_init)
    o_diff = differential_attention(dq, qpos, dk_all, dv_all, kpos, lam)
    o_diff = (rms_norm(o_diff, diff_subln) * (1.0 - lam_init)).reshape(b, s, MIX_W)

    merged = gate_fox * (o_fox @ w_branch[0]) + gate_diff * (o_diff @ w_branch[1])
    h = x + merged @ w_out
    hn = rms_norm(h, norm_ffn).reshape(b * s, D_MODEL)
    h = h + routed_moe(hn, w_router, b_router, moe_w_gate, moe_b_gate, moe_w_up, moe_b_up,
                       moe_w_down, moe_b_down).reshape(b, s, D_MODEL)
    return h, new_state


def setup_inputs(seed: int = 0) -> dict:
    key = jax.random.key(seed)
    keys = list(jax.random.split(key, 32))
    f32 = jnp.float32

    def normal(shape, scale=1.0):
        return jax.random.normal(keys.pop(), shape, f32) * scale

    d, e, f = D_MODEL, N_EXPERTS, D_EXPERT
    return {
        'x_prompt': normal((BATCH, SEQ, d)),
        'x_sample': normal((DEC_BATCH, DEC_SEQ, d)),
        'cache_fox_k': normal((DEPTH, DEC_BATCH, PAST_LEN, H_FOX, DH_FOX)),
        'cache_fox_v': normal((DEPTH, DEC_BATCH, PAST_LEN, H_FOX, DH_FOX)),
        'cache_fox_logf': jax.nn.log_sigmoid(FORGET_BIAS_INIT + normal((DEPTH, DEC_BATCH, PAST_LEN, H_FOX))),
        'cache_diff_k': normal((DEPTH, DEC_BATCH, PAST_LEN, H_DIFF, 2 * DH_DIFF)),
        'cache_diff_v': normal((DEPTH, DEC_BATCH, PAST_LEN, H_DIFF, 2 * DH_DIFF)),
        'norm_mix': 1.0 + normal((DEPTH, d), 0.1),
        'w_in': normal((DEPTH, d, N_IN), d ** -0.5),
        'b_forget': FORGET_BIAS_INIT + normal((DEPTH, H_FOX), 0.5),
        'lambda_q1': normal((DEPTH, DH_DIFF), 0.1),
        'lambda_k1': normal((DEPTH, DH_DIFF), 0.1),
        'lambda_q2': normal((DEPTH, DH_DIFF), 0.1),
        'lambda_k2': normal((DEPTH, DH_DIFF), 0.1),
        'diff_subln': 1.0 + normal((DEPTH, 2 * DH_DIFF), 0.1),
        'w_branch': normal((DEPTH, 2, MIX_W, d), MIX_W ** -0.5),
        'w_out': normal((DEPTH, d, d), d ** -0.5),
        'norm_ffn': 1.0 + normal((DEPTH, d), 0.1),
        'w_router': normal((DEPTH, d, e), d ** -0.5),
        'b_router': normal((DEPTH, e), 0.01),
        'moe_w_gate': normal((DEPTH, e, d, f), d ** -0.5),
        'moe_b_gate': normal((DEPTH, e, f), 0.01),
        'moe_w_up': normal((DEPTH, e, d, f), d ** -0.5),
        'moe_b_up': normal((DEPTH, e, f), 0.01),
        'moe_w_down': normal((DEPTH, e, f, d), f ** -0.5),
        'moe_b_down': normal((DEPTH, e, d), 0.01),
        'norm_final': 1.0 + normal((d,), 0.1),
    }


def reference(x_prompt, x_sample, cache_fox_k, cache_fox_v, cache_fox_logf, cache_diff_k, cache_diff_v,
              norm_mix, w_in, b_forget, lambda_q1, lambda_k1, lambda_q2, lambda_k2, diff_subln,
              w_branch, w_out, norm_ffn, w_router, b_router, moe_w_gate, moe_b_gate, moe_w_up,
              moe_b_up, moe_w_down, moe_b_down, norm_final):
    pos_p = jnp.arange(x_prompt.shape[1], dtype=jnp.int32)
    pos_s = cache_fox_k.shape[2] + jnp.arange(x_sample.shape[1], dtype=jnp.int32)
    hp, hs = x_prompt, x_sample
    st_p, st_s = [], []
    for l in range(DEPTH):
        lw = (norm_mix[l], w_in[l], b_forget[l], lambda_q1[l], lambda_k1[l], lambda_q2[l], lambda_k2[l],
              diff_subln[l], w_branch[l], w_out[l], norm_ffn[l], w_router[l], b_router[l],
              moe_w_gate[l], moe_b_gate[l], moe_w_up[l], moe_b_up[l], moe_w_down[l], moe_b_down[l])
        hp, sp = hybrid_layer(hp, pos_p, None, l, *lw)
        past = (cache_fox_k[l], cache_fox_v[l], cache_fox_logf[l], cache_diff_k[l], cache_diff_v[l])
        hs, ss = hybrid_layer(hs, pos_s, past, l, *lw)
        st_p.append(sp)
        st_s.append(ss)
    y_prompt = rms_norm(hp, norm_final)
    y_sample = rms_norm(hs, norm_final)
    p_fox_k = jnp.stack([st[0] for st in st_p])
    p_fox_v = jnp.stack([st[1] for st in st_p])
    p_fox_logf = jnp.stack([st[2] for st in st_p])
    p_diff_k = jnp.stack([st[3] for st in st_p])
    p_diff_v = jnp.stack([st[4] for st in st_p])
    s_fox_k = jnp.stack([st[0] for st in st_s])
    s_fox_v = jnp.stack([st[1] for st in st_s])
    s_fox_logf = jnp.stack([st[2] for st in st_s])
    s_diff_k = jnp.stack([st[3] for st in st_s])
    s_diff_v = jnp.stack([st[4] for st in st_s])
    return (y_prompt, y_sample, p_fox_k, p_fox_v, p_fox_logf, p_diff_k, p_diff_v,
            s_fox_k, s_fox_v, s_fox_logf, s_diff_k, s_diff_v)
```

```python
import functools
import math

import jax
import jax.numpy as jnp
from jax import lax
from jax.experimental import pallas as pl
from jax.experimental.pallas import tpu as pltpu

F32 = jnp.float32
BF16 = jnp.bfloat16

CHUNK = 64
ROPE_THETA = 500000.0
TOP_K = 4
SWIGLU_LIMIT = 7.0
SWIGLU_ALPHA = 1.702
MOE_BLOCK = 256
NORM_EPS = 1e-5
LANE = 128
HEAD_W = 128
VMEM_LIMIT = 56 * 1024 * 1024
NEG_INIT = -1e30


def _tile(n, pref):
    if n <= LANE:
        return n
    t = min(pref, n) // LANE * LANE
    while n % t:
        t -= LANE
    return t


def _params(sem):
    return pltpu.CompilerParams(dimension_semantics=sem, vmem_limit_bytes=VMEM_LIMIT)


def _rmsnorm_rows(x, g):
    ms = jnp.mean(x * x, axis=-1, keepdims=True)
    return x * lax.rsqrt(ms + NORM_EPS) * g


def _norm_pair_kernel(xp_ref, xs_ref, g_ref, o_ref, *, n_prompt_tiles):
    i = pl.program_id(0)

    @pl.when(i < n_prompt_tiles)
    def _():
        o_ref[...] = _rmsnorm_rows(xp_ref[...], g_ref[...]).astype(o_ref.dtype)

    @pl.when(i >= n_prompt_tiles)
    def _():
        o_ref[...] = _rmsnorm_rows(xs_ref[...], g_ref[...]).astype(o_ref.dtype)


def _norm_pair(xp, xs, g, tm):
    tp, d = xp.shape
    ts = xs.shape[0]
    npt, nst = tp // tm, ts // tm
    return pl.pallas_call(
        functools.partial(_norm_pair_kernel, n_prompt_tiles=npt),
        grid=(npt + nst,),
        in_specs=[
            pl.BlockSpec((tm, d), lambda i: (jnp.minimum(i, npt - 1), 0)),
            pl.BlockSpec((tm, d), lambda i: (jnp.maximum(i - npt, 0), 0)),
            pl.BlockSpec((1, d), lambda i: (0, 0)),
        ],
        out_specs=pl.BlockSpec((tm, d), lambda i: (i, 0)),
        out_shape=jax.ShapeDtypeStruct((tp + ts, d), BF16),
        compiler_params=_params(("arbitrary",)),
        name="norm_mix",
    )(xp, xs, g.reshape(1, d))


def _proj_kernel(a_ref, w_ref, *rest, mode):
    z = jnp.dot(a_ref[...], w_ref[...], preferred_element_type=F32)
    if mode == "plain":
        (o_ref,) = rest
        o_ref[...] = z
    elif mode == "sigmoid":
        (o_ref,) = rest
        o_ref[...] = jax.nn.sigmoid(z)
    elif mode == "logsig":
        b_ref, o_ref = rest
        o_ref[...] = jax.nn.log_sigmoid(z + b_ref[...])
    else:
        c_ref, s_ref, o_ref = rest
        tm, tn = z.shape
        lane = lax.broadcasted_iota(jnp.int32, (tm, LANE), 1)
        first_half = (lane % 64) < 8
        cos, sin = c_ref[...], s_ref[...]
        for c in range(tn // LANE):
            zc = z[:, c * LANE:(c + 1) * LANE]
            rot = jnp.where(first_half, pltpu.roll(zc, LANE - 8, 1), pltpu.roll(zc, 8, 1))
            o_ref[:, c * LANE:(c + 1) * LANE] = zc * cos + rot * sin


def _proj(a, w, n_cols, mode, extra=(), tm=1024, tn=512, name="proj"):
    t, d = a.shape
    tm = _tile(t, tm)
    tn = _tile(n_cols, tn)
    in_specs = [
        pl.BlockSpec((tm, d), lambda i, j: (i, 0)),
        pl.BlockSpec((d, tn), lambda i, j: (0, j)),
    ]
    if mode == "logsig":
        in_specs.append(pl.BlockSpec((1, tn), lambda i, j: (0, j)))
    elif mode == "rope":
        in_specs += [pl.BlockSpec((tm, LANE), lambda i, j: (i, 0))] * 2
    return pl.pallas_call(
        functools.partial(_proj_kernel, mode=mode),
        grid=(t // tm, n_cols // tn),
        in_specs=in_specs,
        out_specs=pl.BlockSpec((tm, tn), lambda i, j: (i, j)),
        out_shape=jax.ShapeDtypeStruct((t, n_cols), F32),
        compiler_params=_params(("arbitrary", "arbitrary")),
        name=name,
    )(a, w, *extra)


def _online_update(s, vb, m_ref, l_ref, acc_ref):
    m_prev = m_ref[...]
    m_new = jnp.maximum(m_prev, jnp.max(s, axis=-1, keepdims=True))
    alpha = jnp.exp(m_prev - m_new)
    p = jnp.exp(s - m_new)
    l_ref[...] = alpha * l_ref[...] + jnp.sum(p, axis=-1, keepdims=True)
    acc_ref[...] = alpha * acc_ref[...] + jnp.dot(p.astype(BF16), vb, preferred_element_type=F32)
    m_ref[...] = m_new


def _qk(q, k):
    return lax.dot_general(q, k, (((1,), (1,)), ((), ())), preferred_element_type=F32)


def _pick_lane(x, idx):
    lane = lax.broadcasted_iota(jnp.int32, x.shape, 1)
    return jnp.sum(jnp.where(lane == idx, x, 0.0), axis=1, keepdims=True)


def _diff_finish(o1, o2, lam, g, post_scale):
    o = o1 - lam * o2
    return _rmsnorm_rows(o, g) * post_scale


def _fox_prompt_kernel(q_ref, k_ref, v_ref, cq_ref, ck_ref, o_ref, m_s, l_s, acc_s, *, tq, scale):
    h = pl.program_id(1)
    qi = pl.program_id(2)
    q = (q_ref[...] * scale).astype(BF16)
    cq = _pick_lane(cq_ref[...], h)
    m_s[...] = jnp.full(m_s.shape, NEG_INIT, F32)
    l_s[...] = jnp.zeros(l_s.shape, F32)
    acc_s[...] = jnp.zeros(acc_s.shape, F32)

    def tile(kj, diagonal):
        ks = pl.multiple_of(kj * tq, tq)
        kb = k_ref[pl.ds(ks, tq), :].astype(BF16)
        vb = v_ref[pl.ds(ks, tq), :].astype(BF16)
        s = _qk(q, kb) + cq - ck_ref[kj, pl.ds(h, 1), :]
        if diagonal:
            r = lax.broadcasted_iota(jnp.int32, s.shape, 0)
            c = lax.broadcasted_iota(jnp.int32, s.shape, 1)
            s = jnp.where(c <= r, s, -jnp.inf)
        _online_update(s, vb, m_s, l_s, acc_s)

    def body(kj, carry):
        tile(kj, False)
        return carry

    lax.fori_loop(0, qi, body, 0)
    tile(qi, True)
    o_ref[...] = (acc_s[...] / l_s[...]).astype(o_ref.dtype)


def _diff_prompt_kernel(lam_ref, q_ref, k_ref, v_ref, g_ref, o_ref, m1, l1, a1, m2, l2, a2,
                        *, tq, scale, post_scale):
    qi = pl.program_id(2)
    q = q_ref[...] * scale
    lane = lax.broadcasted_iota(jnp.int32, q.shape, 1)
    q1 = jnp.where(lane < HEAD_W // 2, q, 0.0).astype(BF16)
    q2 = jnp.where(lane >= HEAD_W // 2, q, 0.0).astype(BF16)
    for m_s, l_s, a_s in ((m1, l1, a1), (m2, l2, a2)):
        m_s[...] = jnp.full(m_s.shape, NEG_INIT, F32)
        l_s[...] = jnp.zeros(l_s.shape, F32)
        a_s[...] = jnp.zeros(a_s.shape, F32)

    def tile(kj, diagonal):
        ks = pl.multiple_of(kj * tq, tq)
        kb = k_ref[pl.ds(ks, tq), :].astype(BF16)
        vb = v_ref[pl.ds(ks, tq), :].astype(BF16)
        s1 = _qk(q1, kb)
        s2 = _qk(q2, kb)
        if diagonal:
            r = lax.broadcasted_iota(jnp.int32, s1.shape, 0)
            c = lax.broadcasted_iota(jnp.int32, s1.shape, 1)
            visible = (c // CHUNK) <= (r // CHUNK)
            s1 = jnp.where(visible, s1, -jnp.inf)
            s2 = jnp.where(visible, s2, -jnp.inf)
        _online_update(s1, vb, m1, l1, a1)
        _online_update(s2, vb, m2, l2, a2)

    def body(kj, carry):
        tile(kj, False)
        return carry

    lax.fori_loop(0, qi, body, 0)
    tile(qi, True)
    o = _diff_finish(a1[...] / l1[...], a2[...] / l2[...], lam_ref[0], g_ref[...], post_scale)
    o_ref[...] = o.astype(o_ref.dtype)


def _prompt_attention(mode, q_arr, q_col0, k_arr, k_col0, v_arr, v_col0, batch, seq, n_heads,
                      extra, scale, post_scale=None):
    tq = _tile(seq, 512)
    nq = seq // tq
    row_q = lambda b, h, qi: (b * nq + qi, q_col0 + h)
    kv_spec = lambda col0: pl.BlockSpec((seq, HEAD_W), lambda b, h, qi: (b, col0 + h))
    col = pltpu.VMEM((tq, 1), F32)
    acc = pltpu.VMEM((tq, HEAD_W), F32)
    if mode == "fox":
        cq, ck = extra
        kern = functools.partial(_fox_prompt_kernel, tq=tq, scale=scale)
        in_specs = [
            pl.BlockSpec((tq, HEAD_W), row_q), kv_spec(k_col0), kv_spec(v_col0),
            pl.BlockSpec((tq, n_heads), lambda b, h, qi: (b * nq + qi, 0)),
            pl.BlockSpec((None, nq, n_heads, tq), lambda b, h, qi: (b, 0, 0, 0)),
        ]
        args = (q_arr, k_arr, v_arr, cq, ck)
        scratch = [col, col, acc]
    else:
        lam, g = extra
        kern = functools.partial(_diff_prompt_kernel, tq=tq, scale=scale, post_scale=post_scale)
        in_specs = [
            pl.BlockSpec(memory_space=pltpu.SMEM),
            pl.BlockSpec((tq, HEAD_W), row_q), kv_spec(k_col0), kv_spec(v_col0),
            pl.BlockSpec((1, HEAD_W), lambda b, h, qi: (0, 0)),
        ]
        args = (lam, q_arr, k_arr, v_arr, g)
        scratch = [col, col, acc, col, col, acc]
    return pl.pallas_call(
        kern,
        grid=(batch, n_heads, nq),
        in_specs=in_specs,
        out_specs=pl.BlockSpec((tq, HEAD_W), lambda b, h, qi: (b * nq + qi, h)),
        out_shape=jax.ShapeDtypeStruct((batch * seq, n_heads * HEAD_W), BF16),
        scratch_shapes=scratch,
        compiler_params=_params(("arbitrary", "arbitrary", "arbitrary")),
        name=mode + "_prompt_attn",
    )(*args)


def _fox_sample_kernel(q_ref, kc_ref, vc_ref, kn_ref, vn_ref, cq_ref, ckc_ref, ckn_ref, o_ref,
                       *, heads_per_step, scale):
    hg = pl.program_id(1)
    for hh in range(heads_per_step):
        sl = slice(hh * HEAD_W, (hh + 1) * HEAD_W)
        h = hg * heads_per_step + hh
        q = (q_ref[:, sl] * scale).astype(BF16)
        cq = _pick_lane(cq_ref[...], h)
        sc = _qk(q, kc_ref[:, sl].astype(BF16)) + cq - ckc_ref[pl.ds(h, 1), :]
        sn = _qk(q, kn_ref[:, sl].astype(BF16)) + cq - ckn_ref[pl.ds(h, 1), :]
        r = lax.broadcasted_iota(jnp.int32, sn.shape, 0)
        c = lax.broadcasted_iota(jnp.int32, sn.shape, 1)
        sn = jnp.where(c <= r, sn, -jnp.inf)
        m = jnp.maximum(jnp.max(sc, axis=-1, keepdims=True), jnp.max(sn, axis=-1, keepdims=True))
        pc = jnp.exp(sc - m)
        pn = jnp.exp(sn - m)
        l = jnp.sum(pc, axis=-1, keepdims=True) + jnp.sum(pn, axis=-1, keepdims=True)
        o = (jnp.dot(pc.astype(BF16), vc_ref[:, sl].astype(BF16), preferred_element_type=F32)
             + jnp.dot(pn.astype(BF16), vn_ref[:, sl].astype(BF16), preferred_element_type=F32))
        o_ref[:, sl] = (o / l).astype(o_ref.dtype)


def _diff_sample_kernel(lam_ref, q_ref, kc_ref, vc_ref, kn_ref, vn_ref, g_ref, o_ref,
                        *, heads_per_step, scale, post_scale):
    for hh in range(heads_per_step):
        sl = slice(hh * HEAD_W, (hh + 1) * HEAD_W)
        q = q_ref[:, sl] * scale
        lane = lax.broadcasted_iota(jnp.int32, q.shape, 1)
        kc = kc_ref[:, sl].astype(BF16)
        kn = kn_ref[:, sl].astype(BF16)
        vc = vc_ref[:, sl].astype(BF16)
        vn = vn_ref[:, sl].astype(BF16)
        outs = []
        for keep in (lane < HEAD_W // 2, lane >= HEAD_W // 2):
            qh = jnp.where(keep, q, 0.0).astype(BF16)
            sc = _qk(qh, kc)
            sn = _qk(qh, kn)
            m = jnp.maximum(jnp.max(sc, axis=-1, keepdims=True), jnp.max(sn, axis=-1, keepdims=True))
            pc = jnp.exp(sc - m)
            pn = jnp.exp(sn - m)
            l = jnp.sum(pc, axis=-1, keepdims=True) + jnp.sum(pn, axis=-1, keepdims=True)
            o = (jnp.dot(pc.astype(BF16), vc, preferred_element_type=F32)
                 + jnp.dot(pn.astype(BF16), vn, preferred_element_type=F32))
            outs.append(o / l)
        o = _diff_finish(outs[0], outs[1], lam_ref[0], g_ref[...], post_scale)
        o_ref[:, sl] = o.astype(o_ref.dtype)


def _sample_attention(mode, q_arr, q_col0, kc_arr, vc_arr, kn_arr, kn_col0, vn_arr, vn_col0,
                      row0, batch, seq, past, n_heads, extra, scale, post_scale=None):
    hb = 4 if n_heads % 4 == 0 else 1
    w = hb * HEAD_W
    rb0 = row0 // seq
    new = lambda col0: pl.BlockSpec((seq, w), lambda b, hg: (rb0 + b, col0 // hb + hg))
    cache = pl.BlockSpec((None, past, w), lambda b, hg: (b, 0, hg))
    if mode == "fox":
        cq, ckc, ckn = extra
        kern = functools.partial(_fox_sample_kernel, heads_per_step=hb, scale=scale)
        in_specs = [
            new(q_col0), cache, cache, new(kn_col0), new(vn_col0),
            pl.BlockSpec((seq, n_heads), lambda b, hg: (b, 0)),
            pl.BlockSpec((None, n_heads, past), lambda b, hg: (b, 0, 0)),
            pl.BlockSpec((None, n_heads, seq), lambda b, hg: (b, 0, 0)),
        ]
        args = (q_arr, kc_arr, vc_arr, kn_arr, vn_arr, cq, ckc, ckn)
    else:
        lam, g = extra
        kern = functools.partial(_diff_sample_kernel, heads_per_step=hb, scale=scale,
                                 post_scale=post_scale)
        in_specs = [
            pl.BlockSpec(memory_space=pltpu.SMEM),
            new(q_col0), cache, cache, new(kn_col0), new(vn_col0),
            pl.BlockSpec((1, HEAD_W), lambda b, hg: (0, 0)),
        ]
        args = (lam, q_arr, kc_arr, vc_arr, kn_arr, vn_arr, g)
    return pl.pallas_call(
        kern,
        grid=(batch, n_heads // hb),
        in_specs=in_specs,
        out_specs=pl.BlockSpec((seq, w), lambda b, hg: (b, hg)),
        out_shape=jax.ShapeDtypeStruct((batch * seq, n_heads * HEAD_W), BF16),
        compiler_params=_params(("arbitrary", "arbitrary")),
        name=mode + "_sample_attn",
    )(*args)


def _merge_kernel(of_ref, od_ref, wf_ref, wd_ref, gf_ref, gd_ref, o_ref):
    a = jnp.dot(of_ref[...], wf_ref[...], preferred_element_type=F32)
    b = jnp.dot(od_ref[...], wd_ref[...], preferred_element_type=F32)
    o_ref[...] = (gf_ref[...] * a + gd_ref[...] * b).astype(o_ref.dtype)


def _merge(o_fox, o_diff, wb_fox, wb_diff, gates, tm=1024, tn=512):
    t, mix = o_fox.shape
    d = wb_fox.shape[1]
    tm, tn = _tile(t, tm), _tile(d, tn)
    nj = d // tn
    return pl.pallas_call(
        _merge_kernel,
        grid=(t // tm, nj),
        in_specs=[
            pl.BlockSpec((tm, mix), lambda i, j: (i, 0)),
            pl.BlockSpec((tm, mix), lambda i, j: (i, 0)),
            pl.BlockSpec((mix, tn), lambda i, j: (0, j)),
            pl.BlockSpec((mix, tn), lambda i, j: (0, j)),
            pl.BlockSpec((tm, tn), lambda i, j: (i, j)),
            pl.BlockSpec((tm, tn), lambda i, j: (i, nj + j)),
        ],
        out_specs=pl.BlockSpec((tm, tn), lambda i, j: (i, j)),
        out_shape=jax.ShapeDtypeStruct((t, d), BF16),
        compiler_params=_params(("arbitrary", "arbitrary")),
        name="branch_merge",
    )(o_fox, o_diff, wb_fox, wb_diff, gates, gates)


def _out_proj_kernel(a_ref, w_ref, xp_ref, xs_ref, o_ref, *, n_prompt_tiles):
    i = pl.program_id(0)
    z = jnp.dot(a_ref[...], w_ref[...], preferred_element_type=F32)

    @pl.when(i < n_prompt_tiles)
    def _():
        o_ref[...] = xp_ref[...] + z

    @pl.when(i >= n_prompt_tiles)
    def _():
        o_ref[...] = xs_ref[...] + z


def _out_proj(merged, w_out, xp, xs, tm=1024, tn=512):
    t, d = merged.shape
    tp, ts = xp.shape[0], xs.shape[0]
    tm = _tile(math.gcd(tp, ts), tm)
    tn = _tile(d, tn)
    npt = tp // tm
    return pl.pallas_call(
        functools.partial(_out_proj_kernel, n_prompt_tiles=npt),
        grid=(t // tm, d // tn),
        in_specs=[
            pl.BlockSpec((tm, d), lambda i, j: (i, 0)),
            pl.BlockSpec((d, tn), lambda i, j: (0, j)),
            pl.BlockSpec((tm, tn), lambda i, j: (jnp.minimum(i, npt - 1), j)),
            pl.BlockSpec((tm, tn), lambda i, j: (jnp.maximum(i - npt, 0), j)),
        ],
        out_specs=pl.BlockSpec((tm, tn), lambda i, j: (i, j)),
        out_shape=jax.ShapeDtypeStruct((t, d), F32),
        compiler_params=_params(("arbitrary", "arbitrary")),
        name="out_proj",
    )(merged, w_out, xp, xs)


def _router_kernel(h_ref, g_ref, w_ref, b_ref, o_ref):
    hn = _rmsnorm_rows(h_ref[...], g_ref[...])
    o_ref[...] = jnp.dot(hn, w_ref[...], preferred_element_type=F32,
                         precision=lax.Precision.HIGHEST) + b_ref[...]


def _router(h1, g, w_pad, b_pad, tm=256):
    t, d = h1.shape
    tm = _tile(t, tm)
    n = w_pad.shape[1]
    return pl.pallas_call(
        _router_kernel,
        grid=(t // tm,),
        in_specs=[
            pl.BlockSpec((tm, d), lambda i: (i, 0)),
            pl.BlockSpec((1, d), lambda i: (0, 0)),
            pl.BlockSpec((d, n), lambda i: (0, 0)),
            pl.BlockSpec((1, n), lambda i: (0, 0)),
        ],
        out_specs=pl.BlockSpec((tm, n), lambda i: (i, 0)),
        out_shape=jax.ShapeDtypeStruct((t, n), F32),
        compiler_params=_params(("arbitrary",)),
        name="router",
    )(h1, g.reshape(1, d), w_pad, b_pad)


def _gather_kernel(tok_ref, nu_ref, h_hbm, g_ref, o_ref, buf, sem):
    i = pl.program_id(0)
    n_used = nu_ref[0]

    def row_copy(tok, slot, r):
        return pltpu.make_async_copy(h_hbm.at[pl.ds(tok, 1), :], buf.at[slot, pl.ds(r, 1), :],
                                     sem.at[slot])

    def start_block(blk, slot):
        def body(r, carry):
            row_copy(tok_ref[blk * MOE_BLOCK + r], slot, r).start()
            return carry
        lax.fori_loop(0, MOE_BLOCK, body, 0)

    def wait_block(slot):
        def body(r, carry):
            row_copy(0, slot, r).wait()
            return carry
        lax.fori_loop(0, MOE_BLOCK, body, 0)

    @pl.when(i == 0)
    def _():
        start_block(0, 0)

    @pl.when(i + 1 < n_used)
    def _():
        start_block(i + 1, (i + 1) % 2)

    @pl.when(i < n_used)
    def _():
        slot = i % 2
        wait_block(slot)
        o_ref[...] = _rmsnorm_rows(buf[slot], g_ref[...]).astype(o_ref.dtype)

    @pl.when(i >= n_used)
    def _():
        o_ref[...] = jnp.zeros(o_ref.shape, o_ref.dtype)


def _gather_norm(h1, g, row_tok, n_used, n_blocks):
    d = h1.shape[1]
    return pl.pallas_call(
        _gather_kernel,
        grid_spec=pltpu.PrefetchScalarGridSpec(
            num_scalar_prefetch=2,
            grid=(n_blocks,),
            in_specs=[
                pl.BlockSpec(memory_space=pl.ANY),
                pl.BlockSpec((1, d), lambda i, tok, nu: (0, 0)),
            ],
            out_specs=pl.BlockSpec((MOE_BLOCK, d), lambda i, tok, nu: (i, 0)),
            scratch_shapes=[pltpu.VMEM((2, MOE_BLOCK, d), F32), pltpu.SemaphoreType.DMA((2,))],
        ),
        out_shape=jax.ShapeDtypeStruct((n_blocks * MOE_BLOCK, d), BF16),
        compiler_params=_params(("arbitrary",)),
        name="moe_gather",
    )(row_tok, n_used, h1, g.reshape(1, d))


def _is_first_of_expert(be_ref, i):
    return (i == 0) | (be_ref[i] != be_ref[jnp.maximum(i - 1, 0)])


def _moe_up_kernel(be_ref, nu_ref, x_ref, wg_ref, wu_ref, bg_ref, bu_ref, h_ref, wg_s, wu_s):
    i = pl.program_id(1)
    active = i < nu_ref[0]

    @pl.when(active & _is_first_of_expert(be_ref, i))
    def _():
        wg_s[...] = wg_ref[...].astype(BF16)
        wu_s[...] = wu_ref[...].astype(BF16)

    @pl.when(active)
    def _():
        x = x_ref[...]
        g = jnp.dot(x, wg_s[...], preferred_element_type=F32) + bg_ref[...]
        u = jnp.dot(x, wu_s[...], preferred_element_type=F32) + bu_ref[...]
        g = jnp.minimum(g, SWIGLU_LIMIT)
        u = jnp.clip(u, -SWIGLU_LIMIT, SWIGLU_LIMIT)
        h_ref[...] = ((u + 1.0) * (g * jax.nn.sigmoid(SWIGLU_ALPHA * g))).astype(h_ref.dtype)

    @pl.when(jnp.logical_not(active))
    def _():
        h_ref[...] = jnp.zeros(h_ref.shape, h_ref.dtype)


def _moe_down_kernel(be_ref, nu_ref, h_ref, wd_ref, bd_ref, rw_ref, y_ref, wd_s):
    i = pl.program_id(1)
    active = i < nu_ref[0]

    @pl.when(active & _is_first_of_expert(be_ref, i))
    def _():
        wd_s[...] = wd_ref[...].astype(BF16)

    @pl.when(active)
    def _():
        y = jnp.dot(h_ref[...], wd_s[...], preferred_element_type=F32) + bd_ref[...]
        rw = rw_ref[...]
        for c in range(y.shape[1] // LANE):
            y_ref[:, c * LANE:(c + 1) * LANE] = y[:, c * LANE:(c + 1) * LANE] * rw

    @pl.when(jnp.logical_not(active))
    def _():
        y_ref[...] = jnp.zeros(y_ref.shape, y_ref.dtype)


def _moe_up(xs, w_gate, w_up, b_gate, b_up, block_e, n_used, tn=512):
    n_rows, d = xs.shape
    n_e, _, f = w_gate.shape
    tn = _tile(f, tn)
    nb = n_rows // MOE_BLOCK
    row = lambda j, i, be, nu: (jnp.minimum(i, nu[0] - 1), 0)
    wmap = lambda j, i, be, nu: (be[i], 0, j)
    return pl.pallas_call(
        _moe_up_kernel,
        grid_spec=pltpu.PrefetchScalarGridSpec(
            num_scalar_prefetch=2,
            grid=(f // tn, nb),
            in_specs=[
                pl.BlockSpec((MOE_BLOCK, d), row),
                pl.BlockSpec((None, d, tn), wmap),
                pl.BlockSpec((None, d, tn), wmap),
                pl.BlockSpec((None, 1, tn), wmap),
                pl.BlockSpec((None, 1, tn), wmap),
            ],
            out_specs=pl.BlockSpec((MOE_BLOCK, tn), lambda j, i, be, nu: (i, j)),
            scratch_shapes=[pltpu.VMEM((d, tn), BF16), pltpu.VMEM((d, tn), BF16)],
        ),
        out_shape=jax.ShapeDtypeStruct((n_rows, f), BF16),
        compiler_params=_params(("arbitrary", "arbitrary")),
        name="moe_up",
    )(block_e, n_used, xs, w_gate, w_up, b_gate.reshape(n_e, 1, f), b_up.reshape(n_e, 1, f))


def _moe_down(h, w_down, b_down, row_w_b, block_e, n_used, tn=512):
    n_rows, f = h.shape
    n_e, _, d = w_down.shape
    tn = _tile(d, tn)
    nb = n_rows // MOE_BLOCK
    row = lambda j, i, be, nu: (jnp.minimum(i, nu[0] - 1), 0)
    wmap = lambda j, i, be, nu: (be[i], 0, j)
    return pl.pallas_call(
        _moe_down_kernel,
        grid_spec=pltpu.PrefetchScalarGridSpec(
            num_scalar_prefetch=2,
            grid=(d // tn, nb),
            in_specs=[
                pl.BlockSpec((MOE_BLOCK, f), row),
                pl.BlockSpec((None, f, tn), wmap),
                pl.BlockSpec((None, 1, tn), wmap),
                pl.BlockSpec((MOE_BLOCK, LANE), row),
            ],
            out_specs=pl.BlockSpec((MOE_BLOCK, tn), lambda j, i, be, nu: (i, j)),
            scratch_shapes=[pltpu.VMEM((f, tn), BF16)],
        ),
        out_shape=jax.ShapeDtypeStruct((n_rows, d), F32),
        compiler_params=_params(("arbitrary", "arbitrary")),
        name="moe_down",
    )(block_e, n_used, h, w_down, b_down.reshape(n_e, 1, d), row_w_b)


def _combine_kernel(pos_ref, h_ref, y_hbm, g_ref, o_ref, buf, sem, *, tt):
    i = pl.program_id(0)
    n = pl.num_programs(0)

    def row_copy(p, slot, k, t):
        return pltpu.make_async_copy(y_hbm.at[pl.ds(p, 1), :], buf.at[slot, k, pl.ds(t, 1), :],
                                     sem.at[slot])

    def start_tile(tile, slot):
        def body(r, carry):
            row_copy(pos_ref[tile * (tt * TOP_K) + r], slot, r % TOP_K, r // TOP_K).start()
            return carry
        lax.fori_loop(0, tt * TOP_K, body, 0)

    def wait_tile(slot):
        def body(r, carry):
            row_copy(0, slot, r % TOP_K, r // TOP_K).wait()
            return carry
        lax.fori_loop(0, tt * TOP_K, body, 0)

    @pl.when(i == 0)
    def _():
        start_tile(0, 0)

    @pl.when(i + 1 < n)
    def _():
        start_tile(i + 1, (i + 1) % 2)

    slot = i % 2
    wait_tile(slot)
    acc = h_ref[...]
    for k in range(TOP_K):
        acc = acc + buf[slot, k]
    o_ref[...] = _rmsnorm_rows(acc, g_ref[...])


def _combine(h1, y, pos, g, tt=64):
    t, d = h1.shape
    tt = min(tt, t)
    return pl.pallas_call(
        functools.partial(_combine_kernel, tt=tt),
        grid_spec=pltpu.PrefetchScalarGridSpec(
            num_scalar_prefetch=1,
            grid=(t // tt,),
            in_specs=[
                pl.BlockSpec((tt, d), lambda i, pos: (i, 0)),
                pl.BlockSpec(memory_space=pl.ANY),
                pl.BlockSpec((1, d), lambda i, pos: (0, 0)),
            ],
            out_specs=pl.BlockSpec((tt, d), lambda i, pos: (i, 0)),
            scratch_shapes=[pltpu.VMEM((2, TOP_K, tt, d), F32), pltpu.SemaphoreType.DMA((2,))],
        ),
        out_shape=jax.ShapeDtypeStruct((t, d), F32),
        compiler_params=_params(("arbitrary",)),
        name="moe_combine",
    )(pos, h1, y, g.reshape(1, d))


def _rope_tables(pos):
    half = (HEAD_W // 2) // 8
    inv_freq = ROPE_THETA ** (-jnp.arange(half, dtype=F32) * (2.0 / (2 * half)))
    ang = pos.astype(F32)[:, None] * inv_freq[None, :]
    cos, sin = jnp.cos(ang), jnp.sin(ang)
    ones = jnp.ones((pos.shape[0], HEAD_W // 2 - 2 * half), F32)
    c64 = jnp.concatenate([cos, cos, ones], axis=1)
    s64 = jnp.concatenate([-sin, sin, 0.0 * ones], axis=1)
    return jnp.concatenate([c64, c64], axis=1), jnp.concatenate([s64, s64], axis=1)


def _routing(logits, n_experts, n_blocks):
    t = logits.shape[0]
    top_v, top_i = lax.top_k(logits, TOP_K)
    gate_w = jax.nn.softmax(top_v, axis=-1)
    flat_e = top_i.reshape(-1).astype(jnp.int32)
    flat_t = jnp.repeat(jnp.arange(t, dtype=jnp.int32), TOP_K)
    flat_w = gate_w.reshape(-1)
    order = jnp.argsort(flat_e)
    sorted_e = flat_e[order]
    counts = jnp.zeros((n_experts,), jnp.int32).at[flat_e].add(1)
    starts = jnp.cumsum(counts) - counts
    padded = (counts + MOE_BLOCK - 1) // MOE_BLOCK * MOE_BLOCK
    padded_end = jnp.cumsum(padded)
    padded_start = padded_end - padded
    rank = jnp.arange(t * TOP_K, dtype=jnp.int32) - starts[sorted_e]
    dest = padded_start[sorted_e] + rank
    n_rows = n_blocks * MOE_BLOCK
    row_tok = jnp.zeros((n_rows,), jnp.int32).at[dest].set(flat_t[order])
    row_w = jnp.zeros((n_rows,), F32).at[dest].set(flat_w[order])
    pos = jnp.zeros((t * TOP_K,), jnp.int32).at[order].set(dest)
    n_used = (padded_end[-1] // MOE_BLOCK).astype(jnp.int32)
    blk = jnp.minimum(jnp.arange(n_blocks, dtype=jnp.int32), n_used - 1)
    block_e = jnp.minimum(
        jnp.searchsorted(padded_end, blk * MOE_BLOCK, side="right"), n_experts - 1).astype(jnp.int32)
    return row_tok, row_w, pos, block_e, n_used.reshape(1)


def kernel(x_prompt, x_sample, cache_fox_k, cache_fox_v, cache_fox_logf, cache_diff_k, cache_diff_v, norm_mix, w_in, b_forget, lambda_q1, lambda_k1, lambda_q2, lambda_k2, diff_subln, w_branch, w_out, norm_ffn, w_router, b_router, moe_w_gate, moe_b_gate, moe_w_up, moe_b_up, moe_w_down, moe_b_down, norm_final):
    batch, seq, d = x_prompt.shape
    dbatch, dseq, _ = x_sample.shape
    depth, _, past, n_heads, _ = cache_fox_k.shape
    assert depth == 1 and cache_fox_k.shape[-1] == HEAD_W and cache_diff_k.shape[-1] == HEAD_W
    assert dseq == CHUNK and past % CHUNK == 0, "sample queries must form exactly the newest chunk"
    mix = n_heads * HEAD_W
    n_experts = w_router.shape[-1]
    tp, ts = batch * seq, dbatch * dseq
    t = tp + ts
    assert (t * TOP_K) % MOE_BLOCK == 0
    xp = x_prompt.reshape(tp, d)
    xs = x_sample.reshape(ts, d)

    off_f = 3 * mix
    off_dq = off_f + n_heads
    off_dv = off_dq + 2 * mix
    off_gate = off_dv + mix
    w = w_in[0]
    w_fox = w[:, :off_f].astype(BF16)
    w_fgt = jnp.pad(w[:, off_f:off_dq], ((0, 0), (0, LANE - n_heads))).astype(BF16)
    b_fgt = jnp.pad(b_forget[0], (0, LANE - n_heads)).reshape(1, LANE)
    w_dqk = w[:, off_dq:off_dv].astype(BF16)
    w_dv = w[:, off_dv:off_gate].astype(BF16)
    w_gates = w[:, off_gate:].astype(BF16)

    row_tile = _tile(math.gcd(tp, ts), 1024)
    xn = _norm_pair(xp, xs, norm_mix[0], _tile(math.gcd(tp, ts), 256))

    pos_all = jnp.concatenate([jnp.tile(jnp.arange(seq), batch), jnp.tile(past + jnp.arange(dseq), dbatch)])
    cos_t, sin_t = _rope_tables(pos_all)

    z_fox = _proj(xn, w_fox, 3 * mix, "plain", tm=row_tile, name="proj_fox")
    logf = _proj(xn, w_fgt, LANE, "logsig", extra=(b_fgt,), tm=row_tile, name="proj_forget")[:, :n_heads]
    z_dqk = _proj(xn, w_dqk, 2 * mix, "rope", extra=(cos_t, sin_t), tm=row_tile, name="proj_diff_qk")
    z_dv = _proj(xn, w_dv, mix, "plain", tm=row_tile, name="proj_diff_v")
    gates = _proj(xn, w_gates, 2 * d, "sigmoid", tm=row_tile, name="proj_gates")

    logf_p = logf[:tp].reshape(batch, seq, n_heads)
    logf_s = logf[tp:].reshape(dbatch, dseq, n_heads)
    cum_p = jnp.cumsum(logf_p, axis=1)
    cum_s = jnp.cumsum(jnp.concatenate([cache_fox_logf[0].astype(F32), logf_s], axis=1), axis=1)
    tq = _tile(seq, 512)
    ck_p = cum_p.transpose(0, 2, 1).reshape(batch, n_heads, seq // tq, tq).transpose(0, 2, 1, 3)
    cq_s = cum_s[:, past:].reshape(ts, n_heads)
    ck_s = cum_s.transpose(0, 2, 1)

    f32 = F32
    lam_init = 0.8 - 0.6 * math.exp(-0.3 * 0)
    lam = (jnp.exp(jnp.sum(lambda_q1[0].astype(f32) * lambda_k1[0].astype(f32)))
           - jnp.exp(jnp.sum(lambda_q2[0].astype(f32) * lambda_k2[0].astype(f32))) + lam_init).reshape(1)
    subln = diff_subln[0].reshape(1, HEAD_W)
    nh = n_heads
    fox_scale = HEAD_W ** -0.5
    diff_scale = (HEAD_W // 2) ** -0.5

    o_fox_p = _prompt_attention("fox", z_fox, 0, z_fox, nh, z_fox, 2 * nh, batch, seq, nh,
                                (cum_p.reshape(tp, nh), ck_p), fox_scale)
    o_diff_p = _prompt_attention("diff", z_dqk, 0, z_dqk, nh, z_dv, 0, batch, seq, nh,
                                 (lam, subln), diff_scale, post_scale=1.0 - lam_init)
    o_fox_s = _sample_attention("fox", z_fox, 0, cache_fox_k[0].reshape(dbatch, past, mix),
                                cache_fox_v[0].reshape(dbatch, past, mix), z_fox, nh, z_fox, 2 * nh,
                                tp, dbatch, dseq, past, nh,
                                (cq_s, ck_s[:, :, :past], ck_s[:, :, past:]), fox_scale)
    o_diff_s = _sample_attention("diff", z_dqk, 0, cache_diff_k[0].reshape(dbatch, past, mix),
                                 cache_diff_v[0].reshape(dbatch, past, mix), z_dqk, nh, z_dv, 0,
                                 tp, dbatch, dseq, past, nh, (lam, subln), diff_scale,
                                 post_scale=1.0 - lam_init)
    o_fox = jnp.concatenate([o_fox_p, o_fox_s], axis=0)
    o_diff = jnp.concatenate([o_diff_p, o_diff_s], axis=0)

    merged = _merge(o_fox, o_diff, w_branch[0, 0].astype(BF16), w_branch[0, 1].astype(BF16), gates,
                    tm=row_tile)
    h1 = _out_proj(merged, w_out[0].astype(BF16), xp, xs, tm=row_tile)

    w_r = jnp.pad(w_router[0], ((0, 0), (0, LANE - n_experts)))
    b_r = jnp.pad(b_router[0].astype(F32), (0, LANE - n_experts)).reshape(1, LANE)
    logits = _router(h1, norm_ffn[0], w_r, b_r)[:, :n_experts]
    n_blocks = (t * TOP_K) // MOE_BLOCK + n_experts
    row_tok, row_w, pos, block_e, n_used = _routing(logits, n_experts, n_blocks)
    xs_sorted = _gather_norm(h1, norm_ffn[0], row_tok, n_used, n_blocks)
    hmid = _moe_up(xs_sorted, moe_w_gate[0], moe_w_up[0], moe_b_gate[0], moe_b_up[0], block_e, n_used)
    row_w_b = jnp.broadcast_to(row_w[:, None], (row_w.shape[0], LANE))
    y_rows = _moe_down(hmid, moe_w_down[0], moe_b_down[0], row_w_b, block_e, n_used)
    y = _combine(h1, y_rows, pos, norm_final)

    def state(arr, col0, width, rows0, b, s):
        return arr[rows0:rows0 + b * s, col0:col0 + width].reshape(1, b, s, n_heads, HEAD_W)

    outs_p = (state(z_fox, mix, mix, 0, batch, seq), state(z_fox, 2 * mix, mix, 0, batch, seq),
              logf_p[None], state(z_dqk, mix, mix, 0, batch, seq), state(z_dv, 0, mix, 0, batch, seq))
    outs_s = (state(z_fox, mix, mix, tp, dbatch, dseq), state(z_fox, 2 * mix, mix, tp, dbatch, dseq),
              logf_s[None], state(z_dqk, mix, mix, tp, dbatch, dseq), state(z_dv, 0, mix, tp, dbatch, dseq))
    return (y[:tp].reshape(batch, seq, d), y[tp:].reshape(dbatch, dseq, d)) + outs_p + outs_s
```

```python
import functools
import math

import jax
import jax.numpy as jnp
from jax import lax
from jax.experimental import pallas as pl
from jax.experimental.pallas import tpu as pltpu

F32 = jnp.float32
BF16 = jnp.bfloat16

CHUNK = 64
ROPE_THETA = 500000.0
TOP_K = 4
SWIGLU_LIMIT = 7.0
SWIGLU_ALPHA = 1.702
MOE_BLOCK = 512
DMA_ISSUE_UNROLL = 8
NORM_ROWS = 16
NORM_EPS = 1e-5
LANE = 128
HEAD_W = 128
VMEM_LIMIT = 56 * 1024 * 1024
NEG_INIT = -1e30
ATTN_STRIP = 256


def _tile(n, pref):
    if n <= LANE:
        return n
    t = min(pref, n) // LANE * LANE
    while n % t:
        t -= LANE
    return t


def _tile_rows(n, pref):
    t = min(pref, n) // 8 * 8
    while n % t:
        t -= 8
    return t


def _params(sem):
    return pltpu.CompilerParams(dimension_semantics=sem, vmem_limit_bytes=VMEM_LIMIT)


def _rmsnorm_rows(x, g):
    ms = jnp.mean(x * x, axis=-1, keepdims=True)
    return x * lax.rsqrt(ms + NORM_EPS) * g


def _norm_pair_kernel(xp_ref, xs_ref, g_ref, o_ref, *, n_prompt_tiles):
    i = pl.program_id(0)

    @pl.when(i < n_prompt_tiles)
    def _():
        o_ref[...] = _rmsnorm_rows(xp_ref[...], g_ref[...]).astype(o_ref.dtype)

    @pl.when(i >= n_prompt_tiles)
    def _():
        o_ref[...] = _rmsnorm_rows(xs_ref[...], g_ref[...]).astype(o_ref.dtype)


def _norm_pair(xp, xs, g, tm):
    tp, d = xp.shape
    ts = xs.shape[0]
    npt, nst = tp // tm, ts // tm
    return pl.pallas_call(
        functools.partial(_norm_pair_kernel, n_prompt_tiles=npt),
        grid=(npt + nst,),
        in_specs=[
            pl.BlockSpec((tm, d), lambda i: (jnp.minimum(i, npt - 1), 0)),
            pl.BlockSpec((tm, d), lambda i: (jnp.maximum(i - npt, 0), 0)),
            pl.BlockSpec((1, d), lambda i: (0, 0)),
        ],
        out_specs=pl.BlockSpec((tm, d), lambda i: (i, 0)),
        out_shape=jax.ShapeDtypeStruct((tp + ts, d), BF16),
        compiler_params=_params(("arbitrary",)),
        name="norm_mix",
    )(xp, xs, g.reshape(1, d))


def _proj_kernel(a_ref, w_ref, *rest, mode):
    z = jnp.dot(a_ref[...], w_ref[...], preferred_element_type=F32)
    if mode == "plain":
        (o_ref,) = rest
        o_ref[...] = z
    elif mode == "sigmoid":
        (o_ref,) = rest
        o_ref[...] = jax.nn.sigmoid(z)
    elif mode == "logsig":
        b_ref, o_ref = rest
        o_ref[...] = jax.nn.log_sigmoid(z + b_ref[...])
    else:
        c_ref, s_ref, o_ref = rest
        tm, tn = z.shape
        lane = lax.broadcasted_iota(jnp.int32, (tm, LANE), 1)
        first_half = (lane % 64) < 8
        cos, sin = c_ref[...], s_ref[...]
        for c in range(tn // LANE):
            zc = z[:, c * LANE:(c + 1) * LANE]
            rot = jnp.where(first_half, pltpu.roll(zc, LANE - 8, 1), pltpu.roll(zc, 8, 1))
            o_ref[:, c * LANE:(c + 1) * LANE] = zc * cos + rot * sin


def _proj(a, w, n_cols, mode, extra=(), tm=1024, tn=512, name="proj"):
    t, d = a.shape
    tm = _tile(t, tm)
    tn = _tile(n_cols, tn)
    in_specs = [
        pl.BlockSpec((tm, d), lambda i, j: (i, 0)),
        pl.BlockSpec((d, tn), lambda i, j: (0, j)),
    ]
    if mode == "logsig":
        in_specs.append(pl.BlockSpec((1, tn), lambda i, j: (0, j)))
    elif mode == "rope":
        in_specs += [pl.BlockSpec((tm, LANE), lambda i, j: (i, 0))] * 2
    return pl.pallas_call(
        functools.partial(_proj_kernel, mode=mode),
        grid=(t // tm, n_cols // tn),
        in_specs=in_specs,
        out_specs=pl.BlockSpec((tm, tn), lambda i, j: (i, j)),
        out_shape=jax.ShapeDtypeStruct((t, n_cols), F32),
        compiler_params=_params(("arbitrary", "arbitrary")),
        name=name,
    )(a, w, *extra)


def _cumsum_kernel(x_ref, o_ref):
    rows, w = x_ref.shape
    lane = lax.broadcasted_iota(jnp.int32, (rows, LANE), 1)
    total = jnp.zeros((rows, 1), F32)
    for c in range(w // LANE):
        v = x_ref[:, c * LANE:(c + 1) * LANE]
        shift = 1
        while shift < LANE:
            v = v + jnp.where(lane >= shift, pltpu.roll(v, shift, 1), 0.0)
            shift *= 2
        v = v + total
        o_ref[:, c * LANE:(c + 1) * LANE] = v
        total = v[:, LANE - 1:LANE]


def _cumsum_seq(x):
    b, s, h = x.shape
    w = -(-s // LANE) * LANE
    xt = jnp.pad(x.transpose(0, 2, 1), ((0, 0), (0, 0), (0, w - s)))
    out = pl.pallas_call(
        _cumsum_kernel,
        grid=(b,),
        in_specs=[pl.BlockSpec((None, h, w), lambda i: (i, 0, 0))],
        out_specs=pl.BlockSpec((None, h, w), lambda i: (i, 0, 0)),
        out_shape=jax.ShapeDtypeStruct((b, h, w), F32),
        compiler_params=_params(("arbitrary",)),
        name="cumsum_logf",
    )(xt)
    return out[:, :, :s]


def _qk(q, k):
    return lax.dot_general(q, k, (((1,), (1,)), ((), ())), preferred_element_type=F32)


def _pick_lane(x, idx):
    lane = lax.broadcasted_iota(jnp.int32, x.shape, 1)
    return jnp.sum(jnp.where(lane == idx, x, 0.0), axis=1, keepdims=True)


def _diff_finish(o1, o2, lam, g, post_scale):
    o = o1 - lam * o2
    return _rmsnorm_rows(o, g) * post_scale


def _prompt_attn_kernel(*refs, mode, tq, rs, hb, scale, post_scale):
    if mode == "fox":
        q_ref, k_ref, v_ref, cq_ref, ck_ref, o_ref, q_s, kT_s, v_s, m_s, acc_s = refs
    else:
        lam_ref, q_ref, k_ref, v_ref, g_ref, o_ref, q_s, kT_s, v_s, m_s, acc_s = refs
    hg = pl.program_id(1)
    qi = pl.program_id(2)
    nk = kT_s.shape[1]
    seq = v_s.shape[1]
    maps_per_head = 1 if mode == "fox" else 2

    @pl.when(qi == 0)
    def _():
        for hh in range(hb):
            sl = slice(hh * HEAD_W, (hh + 1) * HEAD_W)
            v_s[hh, :, :HEAD_W] = v_ref[:, sl].astype(BF16)
            v_s[hh, :, HEAD_W:] = jnp.ones((seq, HEAD_W), BF16)
            for kj in range(nk):
                kT_s[hh, kj] = k_ref[kj * tq:(kj + 1) * tq, sl].T.astype(BF16)

    m_s[...] = jnp.full(m_s.shape, NEG_INIT, F32)
    acc_s[...] = jnp.zeros(acc_s.shape, F32)

    cqs = []
    for hh in range(hb):
        q = q_ref[:, hh * HEAD_W:(hh + 1) * HEAD_W] * scale
        if mode == "fox":
            q_s[hh] = q.astype(BF16)
            cqs.append(_pick_lane(cq_ref[...], hg * hb + hh))
        else:
            lane = lax.broadcasted_iota(jnp.int32, q.shape, 1)
            q_s[2 * hh] = jnp.where(lane < HEAD_W // 2, q, 0.0).astype(BF16)
            q_s[2 * hh + 1] = jnp.where(lane >= HEAD_W // 2, q, 0.0).astype(BF16)

    def tile(kj, diagonal):
        ks = pl.multiple_of(kj * tq, tq)
        for r0 in range(0, tq, rs):
            ncol = r0 + rs if diagonal else tq
            if diagonal:
                r = r0 + lax.broadcasted_iota(jnp.int32, (rs, ncol), 0)
                c = lax.broadcasted_iota(jnp.int32, (rs, ncol), 1)
                visible = (c <= r) if mode == "fox" else ((c // CHUNK) <= (r // CHUNK))
            for mi in range(hb * maps_per_head):
                hh = mi // maps_per_head
                s = jnp.dot(q_s[mi, r0:r0 + rs, :], kT_s[hh, kj, :, :ncol], preferred_element_type=F32)
                if mode == "fox":
                    ck = ck_ref[kj, pl.ds(hg * hb + hh, 1), :]
                    s = s + (cqs[hh][r0:r0 + rs] - ck[:, :ncol])
                if diagonal:
                    s = jnp.where(visible, s, -jnp.inf)
                m_prev = m_s[mi, r0:r0 + rs, :]
                m_new = jnp.maximum(m_prev, jnp.max(s, axis=-1, keepdims=True))
                alpha = jnp.exp(m_prev - m_new)
                p = jnp.exp(s - m_new[:, :1])
                pv = jnp.dot(p.astype(BF16), v_s[hh, pl.ds(ks, ncol), :], preferred_element_type=F32)
                acc_s[mi, r0:r0 + rs, :] = (jnp.concatenate([alpha, alpha], axis=1) * acc_s[mi, r0:r0 + rs, :]
                                            + pv)
                m_s[mi, r0:r0 + rs, :] = m_new

    def body(kj, carry):
        tile(kj, False)
        return carry

    lax.fori_loop(0, qi, body, 0)
    tile(qi, True)

    for hh in range(hb):
        sl = slice(hh * HEAD_W, (hh + 1) * HEAD_W)
        if mode == "fox":
            a = acc_s[hh]
            o = a[:, :HEAD_W] / a[:, HEAD_W:]
        else:
            a1, a2 = acc_s[2 * hh], acc_s[2 * hh + 1]
            o = _diff_finish(a1[:, :HEAD_W] / a1[:, HEAD_W:], a2[:, :HEAD_W] / a2[:, HEAD_W:],
                             lam_ref[0], g_ref[...], post_scale)
        o_ref[:, sl] = o.astype(o_ref.dtype)


def _prompt_attention(mode, q_arr, q_col0, k_arr, k_col0, v_arr, v_col0, batch, seq, n_heads,
                      extra, scale, post_scale=None):
    tq = _tile(seq, 512)
    nq = seq // tq
    hb = 2 if n_heads % 2 == 0 else 1
    w = hb * HEAD_W
    n_maps = hb if mode == "fox" else 2 * hb
    row_q = lambda b, hg, qi: (b * nq + qi, q_col0 // hb + hg)
    kv_spec = lambda col0: pl.BlockSpec((seq, w), lambda b, hg, qi: (b, col0 // hb + hg))
    if mode == "fox":
        cq, ck = extra
        in_specs = [
            pl.BlockSpec((tq, w), row_q), kv_spec(k_col0), kv_spec(v_col0),
            pl.BlockSpec((tq, n_heads), lambda b, hg, qi: (b * nq + qi, 0)),
            pl.BlockSpec((None, nq, n_heads, tq), lambda b, hg, qi: (b, 0, 0, 0)),
        ]
        args = (q_arr, k_arr, v_arr, cq, ck)
    else:
        lam, g = extra
        in_specs = [
            pl.BlockSpec(memory_space=pltpu.SMEM),
            pl.BlockSpec((tq, w), row_q), kv_spec(k_col0), kv_spec(v_col0),
            pl.BlockSpec((1, HEAD_W), lambda b, hg, qi: (0, 0)),
        ]
        args = (lam, q_arr, k_arr, v_arr, g)
    return pl.pallas_call(
        functools.partial(_prompt_attn_kernel, mode=mode, tq=tq, rs=min(ATTN_STRIP, tq), hb=hb,
                          scale=scale, post_scale=post_scale),
        grid=(batch, n_heads // hb, nq),
        in_specs=in_specs,
        out_specs=pl.BlockSpec((tq, w), lambda b, hg, qi: (b * nq + qi, hg)),
        out_shape=jax.ShapeDtypeStruct((batch * seq, n_heads * HEAD_W), BF16),
        scratch_shapes=[
            pltpu.VMEM((n_maps, tq, HEAD_W), BF16),
            pltpu.VMEM((hb, nq, HEAD_W, tq), BF16),
            pltpu.VMEM((hb, seq, 2 * HEAD_W), BF16),
            pltpu.VMEM((n_maps, tq, LANE), F32),
            pltpu.VMEM((n_maps, tq, 2 * HEAD_W), F32),
        ],
        compiler_params=_params(("arbitrary", "arbitrary", "arbitrary")),
        name=mode + "_prompt_attn",
    )(*args)


def _fox_sample_kernel(q_ref, kc_ref, vc_ref, kn_ref, vn_ref, cq_ref, ckc_ref, ckn_ref, o_ref,
                       *, heads_per_step, scale):
    hg = pl.program_id(1)
    for hh in range(heads_per_step):
        sl = slice(hh * HEAD_W, (hh + 1) * HEAD_W)
        h = hg * heads_per_step + hh
        q = (q_ref[:, sl] * scale).astype(BF16)
        cq = _pick_lane(cq_ref[...], h)
        sc = _qk(q, kc_ref[:, sl].astype(BF16)) + cq - ckc_ref[pl.ds(h, 1), :]
        sn = _qk(q, kn_ref[:, sl].astype(BF16)) + cq - ckn_ref[pl.ds(h, 1), :]
        r = lax.broadcasted_iota(jnp.int32, sn.shape, 0)
        c = lax.broadcasted_iota(jnp.int32, sn.shape, 1)
        sn = jnp.where(c <= r, sn, -jnp.inf)
        m = jnp.maximum(jnp.max(sc, axis=-1, keepdims=True), jnp.max(sn, axis=-1, keepdims=True))
        pc = jnp.exp(sc - m)
        pn = jnp.exp(sn - m)
        l = jnp.sum(pc, axis=-1, keepdims=True) + jnp.sum(pn, axis=-1, keepdims=True)
        o = (jnp.dot(pc.astype(BF16), vc_ref[:, sl].astype(BF16), preferred_element_type=F32)
             + jnp.dot(pn.astype(BF16), vn_ref[:, sl].astype(BF16), preferred_element_type=F32))
        o_ref[:, sl] = (o / l).astype(o_ref.dtype)


def _diff_sample_kernel(lam_ref, q_ref, kc_ref, vc_ref, kn_ref, vn_ref, g_ref, o_ref,
                        *, heads_per_step, scale, post_scale):
    for hh in range(heads_per_step):
        sl = slice(hh * HEAD_W, (hh + 1) * HEAD_W)
        q = q_ref[:, sl] * scale
        lane = lax.broadcasted_iota(jnp.int32, q.shape, 1)
        kc = kc_ref[:, sl].astype(BF16)
        kn = kn_ref[:, sl].astype(BF16)
        vc = vc_ref[:, sl].astype(BF16)
        vn = vn_ref[:, sl].astype(BF16)
        outs = []
        for keep in (lane < HEAD_W // 2, lane >= HEAD_W // 2):
            qh = jnp.where(keep, q, 0.0).astype(BF16)
            sc = _qk(qh, kc)
            sn = _qk(qh, kn)
            m = jnp.maximum(jnp.max(sc, axis=-1, keepdims=True), jnp.max(sn, axis=-1, keepdims=True))
            pc = jnp.exp(sc - m)
            pn = jnp.exp(sn - m)
            l = jnp.sum(pc, axis=-1, keepdims=True) + jnp.sum(pn, axis=-1, keepdims=True)
            o = (jnp.dot(pc.astype(BF16), vc, preferred_element_type=F32)
                 + jnp.dot(pn.astype(BF16), vn, preferred_element_type=F32))
            outs.append(o / l)
        o = _diff_finish(outs[0], outs[1], lam_ref[0], g_ref[...], post_scale)
        o_ref[:, sl] = o.astype(o_ref.dtype)


def _sample_attention(mode, q_arr, q_col0, kc_arr, vc_arr, kn_arr, kn_col0, vn_arr, vn_col0,
                      row0, batch, seq, past, n_heads, extra, scale, post_scale=None):
    hb = 4 if n_heads % 4 == 0 else 1
    w = hb * HEAD_W
    rb0 = row0 // seq
    new = lambda col0: pl.BlockSpec((seq, w), lambda b, hg: (rb0 + b, col0 // hb + hg))
    cache = pl.BlockSpec((None, past, w), lambda b, hg: (b, 0, hg))
    if mode == "fox":
        cq, ckc, ckn = extra
        kern = functools.partial(_fox_sample_kernel, heads_per_step=hb, scale=scale)
        in_specs = [
            new(q_col0), cache, cache, new(kn_col0), new(vn_col0),
            pl.BlockSpec((seq, n_heads), lambda b, hg: (b, 0)),
            pl.BlockSpec((None, n_heads, past), lambda b, hg: (b, 0, 0)),
            pl.BlockSpec((None, n_heads, seq), lambda b, hg: (b, 0, 0)),
        ]
        args = (q_arr, kc_arr, vc_arr, kn_arr, vn_arr, cq, ckc, ckn)
    else:
        lam, g = extra
        kern = functools.partial(_diff_sample_kernel, heads_per_step=hb, scale=scale,
                                 post_scale=post_scale)
        in_specs = [
            pl.BlockSpec(memory_space=pltpu.SMEM),
            new(q_col0), cache, cache, new(kn_col0), new(vn_col0),
            pl.BlockSpec((1, HEAD_W), lambda b, hg: (0, 0)),
        ]
        args = (lam, q_arr, kc_arr, vc_arr, kn_arr, vn_arr, g)
    return pl.pallas_call(
        kern,
        grid=(batch, n_heads // hb),
        in_specs=in_specs,
        out_specs=pl.BlockSpec((seq, w), lambda b, hg: (b, hg)),
        out_shape=jax.ShapeDtypeStruct((batch * seq, n_heads * HEAD_W), BF16),
        compiler_params=_params(("arbitrary", "arbitrary")),
        name=mode + "_sample_attn",
    )(*args)


def _merge_kernel(of_ref, od_ref, wf_ref, wd_ref, gf_ref, gd_ref, o_ref):
    a = jnp.dot(of_ref[...], wf_ref[...], preferred_element_type=F32)
    b = jnp.dot(od_ref[...], wd_ref[...], preferred_element_type=F32)
    o_ref[...] = (gf_ref[...] * a + gd_ref[...] * b).astype(o_ref.dtype)


def _merge(o_fox, o_diff, wb_fox, wb_diff, gates, tm=1024, tn=512):
    t, mix = o_fox.shape
    d = wb_fox.shape[1]
    tm, tn = _tile(t, tm), _tile(d, tn)
    nj = d // tn
    return pl.pallas_call(
        _merge_kernel,
        grid=(t // tm, nj),
        in_specs=[
            pl.BlockSpec((tm, mix), lambda i, j: (i, 0)),
            pl.BlockSpec((tm, mix), lambda i, j: (i, 0)),
            pl.BlockSpec((mix, tn), lambda i, j: (0, j)),
            pl.BlockSpec((mix, tn), lambda i, j: (0, j)),
            pl.BlockSpec((tm, tn), lambda i, j: (i, j)),
            pl.BlockSpec((tm, tn), lambda i, j: (i, nj + j)),
        ],
        out_specs=pl.BlockSpec((tm, tn), lambda i, j: (i, j)),
        out_shape=jax.ShapeDtypeStruct((t, d), BF16),
        compiler_params=_params(("arbitrary", "arbitrary")),
        name="branch_merge",
    )(o_fox, o_diff, wb_fox, wb_diff, gates, gates)


def _out_proj_kernel(a_ref, w_ref, xp_ref, xs_ref, o_ref, *, n_prompt_tiles):
    i = pl.program_id(0)
    z = jnp.dot(a_ref[...], w_ref[...], preferred_element_type=F32)

    @pl.when(i < n_prompt_tiles)
    def _():
        o_ref[...] = xp_ref[...] + z

    @pl.when(i >= n_prompt_tiles)
    def _():
        o_ref[...] = xs_ref[...] + z


def _out_proj(merged, w_out, xp, xs, tm=1024, tn=512):
    t, d = merged.shape
    tp, ts = xp.shape[0], xs.shape[0]
    tm = _tile(math.gcd(tp, ts), tm)
    tn = _tile(d, tn)
    npt = tp // tm
    return pl.pallas_call(
        functools.partial(_out_proj_kernel, n_prompt_tiles=npt),
        grid=(t // tm, d // tn),
        in_specs=[
            pl.BlockSpec((tm, d), lambda i, j: (i, 0)),
            pl.BlockSpec((d, tn), lambda i, j: (0, j)),
            pl.BlockSpec((tm, tn), lambda i, j: (jnp.minimum(i, npt - 1), j)),
            pl.BlockSpec((tm, tn), lambda i, j: (jnp.maximum(i - npt, 0), j)),
        ],
        out_specs=pl.BlockSpec((tm, tn), lambda i, j: (i, j)),
        out_shape=jax.ShapeDtypeStruct((t, d), F32),
        compiler_params=_params(("arbitrary", "arbitrary")),
        name="out_proj",
    )(merged, w_out, xp, xs)


def _router_kernel(h_ref, g_ref, w_ref, b_ref, o_ref):
    hn = _rmsnorm_rows(h_ref[...], g_ref[...])
    o_ref[...] = jnp.dot(hn, w_ref[...], preferred_element_type=F32,
                         precision=lax.Precision.HIGHEST) + b_ref[...]


def _router(h1, g, w_pad, b_pad, tm=256):
    t, d = h1.shape
    tm = _tile(t, tm)
    n = w_pad.shape[1]
    return pl.pallas_call(
        _router_kernel,
        grid=(t // tm,),
        in_specs=[
            pl.BlockSpec((tm, d), lambda i: (i, 0)),
            pl.BlockSpec((1, d), lambda i: (0, 0)),
            pl.BlockSpec((d, n), lambda i: (0, 0)),
            pl.BlockSpec((1, n), lambda i: (0, 0)),
        ],
        out_specs=pl.BlockSpec((tm, n), lambda i: (i, 0)),
        out_shape=jax.ShapeDtypeStruct((t, n), F32),
        compiler_params=_params(("arbitrary",)),
        name="router",
    )(h1, g.reshape(1, d), w_pad, b_pad)


def _gather_kernel(tok_ref, nu_ref, h_hbm, g_ref, o_ref, buf, sem):
    i = pl.program_id(0)
    n_used = nu_ref[0]

    def row_copy(tok, slot, r):
        return pltpu.make_async_copy(h_hbm.at[pl.ds(tok, 1), :], buf.at[slot, pl.ds(r, 1), :],
                                     sem.at[slot])

    def start_block(blk, slot):
        def body(r, carry):
            row_copy(tok_ref[blk * MOE_BLOCK + r], slot, r).start()
            return carry
        lax.fori_loop(0, MOE_BLOCK, body, 0, unroll=DMA_ISSUE_UNROLL)

    def wait_block(slot):
        def body(r, carry):
            row_copy(0, slot, r).wait()
            return carry
        lax.fori_loop(0, MOE_BLOCK, body, 0, unroll=DMA_ISSUE_UNROLL)

    @pl.when(i == 0)
    def _():
        start_block(0, 0)

    @pl.when(i + 1 < n_used)
    def _():
        start_block(i + 1, (i + 1) % 2)

    @pl.when(i < n_used)
    def _():
        slot = i % 2
        wait_block(slot)

        def norm_rows(c, carry):
            r0 = pl.multiple_of(c * NORM_ROWS, NORM_ROWS)
            x = buf[slot, pl.ds(r0, NORM_ROWS), :]
            o_ref[pl.ds(r0, NORM_ROWS), :] = _rmsnorm_rows(x, g_ref[...]).astype(o_ref.dtype)
            return carry
        lax.fori_loop(0, MOE_BLOCK // NORM_ROWS, norm_rows, 0, unroll=4)

    @pl.when(i >= n_used)
    def _():
        o_ref[...] = jnp.zeros(o_ref.shape, o_ref.dtype)


def _gather_norm(h1, g, row_tok, n_used, n_blocks):
    d = h1.shape[1]
    return pl.pallas_call(
        _gather_kernel,
        grid_spec=pltpu.PrefetchScalarGridSpec(
            num_scalar_prefetch=2,
            grid=(n_blocks,),
            in_specs=[
                pl.BlockSpec(memory_space=pl.ANY),
                pl.BlockSpec((1, d), lambda i, tok, nu: (0, 0)),
            ],
            out_specs=pl.BlockSpec((MOE_BLOCK, d), lambda i, tok, nu: (i, 0)),
            scratch_shapes=[pltpu.VMEM((2, MOE_BLOCK, d), F32), pltpu.SemaphoreType.DMA((2,))],
        ),
        out_shape=jax.ShapeDtypeStruct((n_blocks * MOE_BLOCK, d), BF16),
        compiler_params=_params(("arbitrary",)),
        name="moe_gather",
    )(row_tok, n_used, h1, g.reshape(1, d))


def _is_first_of_expert(be_ref, i):
    return (i == 0) | (be_ref[i] != be_ref[jnp.maximum(i - 1, 0)])


def _moe_up_kernel(be_ref, nu_ref, x_ref, wg_ref, wu_ref, bg_ref, bu_ref, h_ref, wg_s, wu_s):
    i = pl.program_id(1)
    active = i < nu_ref[0]

    @pl.when(active & _is_first_of_expert(be_ref, i))
    def _():
        wg_s[...] = wg_ref[...].astype(BF16)
        wu_s[...] = wu_ref[...].astype(BF16)

    @pl.when(active)
    def _():
        x = x_ref[...]
        g = jnp.dot(x, wg_s[...], preferred_element_type=F32) + bg_ref[...]
        u = jnp.dot(x, wu_s[...], preferred_element_type=F32) + bu_ref[...]
        g = jnp.minimum(g, SWIGLU_LIMIT)
        u = jnp.clip(u, -SWIGLU_LIMIT, SWIGLU_LIMIT)
        h_ref[...] = ((u + 1.0) * (g * jax.nn.sigmoid(SWIGLU_ALPHA * g))).astype(h_ref.dtype)

    @pl.when(jnp.logical_not(active))
    def _():
        h_ref[...] = jnp.zeros(h_ref.shape, h_ref.dtype)


def _moe_down_kernel(be_ref, nu_ref, h_ref, wd_ref, bd_ref, rw_ref, y_ref, wd_s):
    i = pl.program_id(1)
    active = i < nu_ref[0]

    @pl.when(active & _is_first_of_expert(be_ref, i))
    def _():
        wd_s[...] = wd_ref[...].astype(BF16)

    @pl.when(active)
    def _():
        y = jnp.dot(h_ref[...], wd_s[...], preferred_element_type=F32) + bd_ref[...]
        rw = rw_ref[...]
        for c in range(y.shape[1] // LANE):
            y_ref[:, c * LANE:(c + 1) * LANE] = y[:, c * LANE:(c + 1) * LANE] * rw

    @pl.when(jnp.logical_not(active))
    def _():
        y_ref[...] = jnp.zeros(y_ref.shape, y_ref.dtype)


def _moe_up(xs, w_gate, w_up, b_gate, b_up, block_e, n_used, tn=512):
    n_rows, d = xs.shape
    n_e, _, f = w_gate.shape
    tn = _tile(f, tn)
    nb = n_rows // MOE_BLOCK
    row = lambda j, i, be, nu: (jnp.minimum(i, nu[0] - 1), 0)
    wmap = lambda j, i, be, nu: (be[i], 0, j)
    return pl.pallas_call(
        _moe_up_kernel,
        grid_spec=pltpu.PrefetchScalarGridSpec(
            num_scalar_prefetch=2,
            grid=(f // tn, nb),
            in_specs=[
                pl.BlockSpec((MOE_BLOCK, d), row),
                pl.BlockSpec((None, d, tn), wmap),
                pl.BlockSpec((None, d, tn), wmap),
                pl.BlockSpec((None, 1, tn), wmap),
                pl.BlockSpec((None, 1, tn), wmap),
            ],
            out_specs=pl.BlockSpec((MOE_BLOCK, tn), lambda j, i, be, nu: (i, j)),
            scratch_shapes=[pltpu.VMEM((d, tn), BF16), pltpu.VMEM((d, tn), BF16)],
        ),
        out_shape=jax.ShapeDtypeStruct((n_rows, f), BF16),
        compiler_params=_params(("arbitrary", "arbitrary")),
        name="moe_up",
    )(block_e, n_used, xs, w_gate, w_up, b_gate.reshape(n_e, 1, f), b_up.reshape(n_e, 1, f))


def _moe_down(h, w_down, b_down, row_w_b, block_e, n_used, tn=512):
    n_rows, f = h.shape
    n_e, _, d = w_down.shape
    tn = _tile(d, tn)
    nb = n_rows // MOE_BLOCK
    row = lambda j, i, be, nu: (jnp.minimum(i, nu[0] - 1), 0)
    wmap = lambda j, i, be, nu: (be[i], 0, j)
    return pl.pallas_call(
        _moe_down_kernel,
        grid_spec=pltpu.PrefetchScalarGridSpec(
            num_scalar_prefetch=2,
            grid=(d // tn, nb),
            in_specs=[
                pl.BlockSpec((MOE_BLOCK, f), row),
                pl.BlockSpec((None, f, tn), wmap),
                pl.BlockSpec((None, 1, tn), wmap),
                pl.BlockSpec((MOE_BLOCK, LANE), row),
            ],
            out_specs=pl.BlockSpec((MOE_BLOCK, tn), lambda j, i, be, nu: (i, j)),
            scratch_shapes=[pltpu.VMEM((f, tn), BF16)],
        ),
        out_shape=jax.ShapeDtypeStruct((n_rows, d), F32),
        compiler_params=_params(("arbitrary", "arbitrary")),
        name="moe_down",
    )(block_e, n_used, h, w_down, b_down.reshape(n_e, 1, d), row_w_b)


def _combine_kernel(pos_ref, h_ref, y_hbm, g_ref, op_ref, os_ref, buf, sem, *, tt, n_prompt_tiles):
    i = pl.program_id(0)
    n = pl.num_programs(0)

    def row_copy(p, slot, k, t):
        return pltpu.make_async_copy(y_hbm.at[pl.ds(p, 1), :], buf.at[slot, k, pl.ds(t, 1), :],
                                     sem.at[slot])

    def start_tile(tile, slot):
        def body(t, carry):
            for k in range(TOP_K):
                row_copy(pos_ref[(tile * tt + t) * TOP_K + k], slot, k, t).start()
            return carry
        lax.fori_loop(0, tt, body, 0, unroll=DMA_ISSUE_UNROLL // TOP_K)

    def wait_tile(slot):
        def body(t, carry):
            for k in range(TOP_K):
                row_copy(0, slot, k, t).wait()
            return carry
        lax.fori_loop(0, tt, body, 0, unroll=DMA_ISSUE_UNROLL // TOP_K)

    @pl.when(i == 0)
    def _():
        start_tile(0, 0)

    @pl.when(i + 1 < n)
    def _():
        start_tile(i + 1, (i + 1) % 2)

    slot = i % 2
    wait_tile(slot)

    def finish(o_ref):
        def rows(c, carry):
            r0 = pl.multiple_of(c * 8, 8)
            acc = h_ref[pl.ds(r0, 8), :]
            for k in range(TOP_K):
                acc = acc + buf[slot, k, pl.ds(r0, 8), :]
            o_ref[pl.ds(r0, 8), :] = _rmsnorm_rows(acc, g_ref[...])
            return carry
        lax.fori_loop(0, tt // 8, rows, 0, unroll=4)

    @pl.when(i < n_prompt_tiles)
    def _():
        finish(op_ref)

    @pl.when(i >= n_prompt_tiles)
    def _():
        finish(os_ref)


def _combine(h1, y, pos, g, tp, tt=64):
    t, d = h1.shape
    ts = t - tp
    tt = _tile_rows(math.gcd(tp, ts), tt)
    npt = tp // tt
    return pl.pallas_call(
        functools.partial(_combine_kernel, tt=tt, n_prompt_tiles=npt),
        grid_spec=pltpu.PrefetchScalarGridSpec(
            num_scalar_prefetch=1,
            grid=(t // tt,),
            in_specs=[
                pl.BlockSpec((tt, d), lambda i, pos: (i, 0)),
                pl.BlockSpec(memory_space=pl.ANY),
                pl.BlockSpec((1, d), lambda i, pos: (0, 0)),
            ],
            out_specs=[
                pl.BlockSpec((tt, d), lambda i, pos: (jnp.minimum(i, npt - 1), 0)),
                pl.BlockSpec((tt, d), lambda i, pos: (jnp.maximum(i - npt, 0), 0)),
            ],
            scratch_shapes=[pltpu.VMEM((2, TOP_K, tt, d), F32), pltpu.SemaphoreType.DMA((2,))],
        ),
        out_shape=[jax.ShapeDtypeStruct((tp, d), F32), jax.ShapeDtypeStruct((ts, d), F32)],
        compiler_params=_params(("arbitrary",)),
        name="moe_combine",
    )(pos, h1, y, g.reshape(1, d))


def _rope_tables(pos):
    half = (HEAD_W // 2) // 8
    inv_freq = ROPE_THETA ** (-jnp.arange(half, dtype=F32) * (2.0 / (2 * half)))
    ang = pos.astype(F32)[:, None] * inv_freq[None, :]
    cos, sin = jnp.cos(ang), jnp.sin(ang)
    ones = jnp.ones((pos.shape[0], HEAD_W // 2 - 2 * half), F32)
    c64 = jnp.concatenate([cos, cos, ones], axis=1)
    s64 = jnp.concatenate([-sin, sin, 0.0 * ones], axis=1)
    return jnp.concatenate([c64, c64], axis=1), jnp.concatenate([s64, s64], axis=1)


def _routing(logits, n_experts, n_blocks):
    t = logits.shape[0]
    top_v, top_i = lax.top_k(logits, TOP_K)
    gate_w = jax.nn.softmax(top_v, axis=-1)
    flat_e = top_i.reshape(-1).astype(jnp.int32)
    flat_t = jnp.repeat(jnp.arange(t, dtype=jnp.int32), TOP_K)
    flat_w = gate_w.reshape(-1)
    order = jnp.argsort(flat_e).astype(jnp.int32)
    slot_of = jnp.argsort(order).astype(jnp.int32)
    experts = jnp.arange(n_experts, dtype=jnp.int32)
    counts = jnp.sum((flat_e[:, None] == experts[None, :]).astype(jnp.int32), axis=0)
    starts = jnp.cumsum(counts) - counts
    padded = (counts + MOE_BLOCK - 1) // MOE_BLOCK * MOE_BLOCK
    padded_end = jnp.cumsum(padded)
    padded_start = padded_end - padded
    n_rows = n_blocks * MOE_BLOCK
    rows = jnp.arange(n_rows, dtype=jnp.int32)
    row_e = jnp.minimum(jnp.searchsorted(padded_end, rows, side="right"), n_experts - 1)
    offset = rows - padded_start[row_e]
    valid = offset < counts[row_e]
    assign = order[jnp.clip(starts[row_e] + offset, 0, t * TOP_K - 1)]
    row_tok = jnp.where(valid, flat_t[assign], 0)
    row_w = jnp.where(valid, flat_w[assign], 0.0)
    pos = padded_start[flat_e] + slot_of - starts[flat_e]
    n_used = (padded_end[-1] // MOE_BLOCK).astype(jnp.int32)
    blk = jnp.minimum(jnp.arange(n_blocks, dtype=jnp.int32), n_used - 1)
    block_e = jnp.minimum(
        jnp.searchsorted(padded_end, blk * MOE_BLOCK, side="right"), n_experts - 1).astype(jnp.int32)
    return row_tok, row_w, pos, block_e, n_used.reshape(1)


def kernel(x_prompt, x_sample, cache_fox_k, cache_fox_v, cache_fox_logf, cache_diff_k, cache_diff_v, norm_mix, w_in, b_forget, lambda_q1, lambda_k1, lambda_q2, lambda_k2, diff_subln, w_branch, w_out, norm_ffn, w_router, b_router, moe_w_gate, moe_b_gate, moe_w_up, moe_b_up, moe_w_down, moe_b_down, norm_final):
    batch, seq, d = x_prompt.shape
    dbatch, dseq, _ = x_sample.shape
    depth, _, past, n_heads, _ = cache_fox_k.shape
    assert depth == 1 and cache_fox_k.shape[-1] == HEAD_W and cache_diff_k.shape[-1] == HEAD_W
    assert dseq == CHUNK and past % CHUNK == 0, "sample queries must form exactly the newest chunk"
    mix = n_heads * HEAD_W
    n_experts = w_router.shape[-1]
    tp, ts = batch * seq, dbatch * dseq
    t = tp + ts
    assert (t * TOP_K) % MOE_BLOCK == 0
    xp = x_prompt.reshape(tp, d)
    xs = x_sample.reshape(ts, d)

    off_f = 3 * mix
    off_dq = off_f + n_heads
    off_dv = off_dq + 2 * mix
    off_gate = off_dv + mix
    w = w_in[0]
    w_fox = w[:, :off_f].astype(BF16)
    w_fgt = jnp.pad(w[:, off_f:off_dq], ((0, 0), (0, LANE - n_heads))).astype(BF16)
    b_fgt = jnp.pad(b_forget[0], (0, LANE - n_heads)).reshape(1, LANE)
    w_dqk = w[:, off_dq:off_dv].astype(BF16)
    w_dv = w[:, off_dv:off_gate].astype(BF16)
    w_gates = w[:, off_gate:].astype(BF16)

    row_tile = _tile(math.gcd(tp, ts), 1024)
    xn = _norm_pair(xp, xs, norm_mix[0], _tile(math.gcd(tp, ts), 256))

    pos_all = jnp.concatenate([jnp.tile(jnp.arange(seq), batch), jnp.tile(past + jnp.arange(dseq), dbatch)])
    cos_t, sin_t = _rope_tables(pos_all)

    z_fox = _proj(xn, w_fox, 3 * mix, "plain", tm=row_tile, name="proj_fox")
    logf = _proj(xn, w_fgt, LANE, "logsig", extra=(b_fgt,), tm=row_tile, name="proj_forget")[:, :n_heads]
    z_dqk = _proj(xn, w_dqk, 2 * mix, "rope", extra=(cos_t, sin_t), tm=row_tile, name="proj_diff_qk")
    z_dv = _proj(xn, w_dv, mix, "plain", tm=row_tile, name="proj_diff_v")
    gates = _proj(xn, w_gates, 2 * d, "sigmoid", tm=row_tile, name="proj_gates")

    logf_p = logf[:tp].reshape(batch, seq, n_heads)
    logf_s = logf[tp:].reshape(dbatch, dseq, n_heads)
    ck_p_flat = _cumsum_seq(logf_p)
    ck_s = _cumsum_seq(jnp.concatenate([cache_fox_logf[0].astype(F32), logf_s], axis=1))
    cum_p = ck_p_flat.transpose(0, 2, 1)
    tq = _tile(seq, 512)
    ck_p = ck_p_flat.reshape(batch, n_heads, seq // tq, tq).transpose(0, 2, 1, 3)
    cq_s = ck_s[:, :, past:].transpose(0, 2, 1).reshape(ts, n_heads)

    f32 = F32
    lam_init = 0.8 - 0.6 * math.exp(-0.3 * 0)
    lam = (jnp.exp(jnp.sum(lambda_q1[0].astype(f32) * lambda_k1[0].astype(f32)))
           - jnp.exp(jnp.sum(lambda_q2[0].astype(f32) * lambda_k2[0].astype(f32))) + lam_init).reshape(1)
    subln = diff_subln[0].reshape(1, HEAD_W)
    nh = n_heads
    fox_scale = HEAD_W ** -0.5
    diff_scale = (HEAD_W // 2) ** -0.5

    o_fox_p = _prompt_attention("fox", z_fox, 0, z_fox, nh, z_fox, 2 * nh, batch, seq, nh,
                                (cum_p.reshape(tp, nh), ck_p), fox_scale)
    o_diff_p = _prompt_attention("diff", z_dqk, 0, z_dqk, nh, z_dv, 0, batch, seq, nh,
                                 (lam, subln), diff_scale, post_scale=1.0 - lam_init)
    o_fox_s = _sample_attention("fox", z_fox, 0, cache_fox_k[0].reshape(dbatch, past, mix),
                                cache_fox_v[0].reshape(dbatch, past, mix), z_fox, nh, z_fox, 2 * nh,
                                tp, dbatch, dseq, past, nh,
                                (cq_s, ck_s[:, :, :past], ck_s[:, :, past:]), fox_scale)
    o_diff_s = _sample_attention("diff", z_dqk, 0, cache_diff_k[0].reshape(dbatch, past, mix),
                                 cache_diff_v[0].reshape(dbatch, past, mix), z_dqk, nh, z_dv, 0,
                                 tp, dbatch, dseq, past, nh, (lam, subln), diff_scale,
                                 post_scale=1.0 - lam_init)
    o_fox = jnp.concatenate([o_fox_p, o_fox_s], axis=0)
    o_diff = jnp.concatenate([o_diff_p, o_diff_s], axis=0)

    merged = _merge(o_fox, o_diff, w_branch[0, 0].astype(BF16), w_branch[0, 1].astype(BF16), gates,
                    tm=row_tile)
    h1 = _out_proj(merged, w_out[0].astype(BF16), xp, xs, tm=row_tile)

    w_r = jnp.pad(w_router[0], ((0, 0), (0, LANE - n_experts)))
    b_r = jnp.pad(b_router[0].astype(F32), (0, LANE - n_experts)).reshape(1, LANE)
    logits = _router(h1, norm_ffn[0], w_r, b_r)[:, :n_experts]
    n_blocks = (t * TOP_K) // MOE_BLOCK + n_experts
    row_tok, row_w, pos, block_e, n_used = _routing(logits, n_experts, n_blocks)
    xs_sorted = _gather_norm(h1, norm_ffn[0], row_tok, n_used, n_blocks)
    hmid = _moe_up(xs_sorted, moe_w_gate[0], moe_w_up[0], moe_b_gate[0], moe_b_up[0], block_e, n_used)
    row_w_b = jnp.broadcast_to(row_w[:, None], (row_w.shape[0], LANE))
    y_rows = _moe_down(hmid, moe_w_down[0], moe_b_down[0], row_w_b, block_e, n_used)
    y_p, y_s = _combine(h1, y_rows, pos, norm_final, tp)

    def state(arr, col0, width, rows0, b, s):
        return arr[rows0:rows0 + b * s, col0:col0 + width].reshape(1, b, s, n_heads, HEAD_W)

    outs_p = (state(z_fox, mix, mix, 0, batch, seq), state(z_fox, 2 * mix, mix, 0, batch, seq),
              logf_p[None], state(z_dqk, mix, mix, 0, batch, seq), state(z_dv, 0, mix, 0, batch, seq))
    outs_s = (state(z_fox, mix, mix, tp, dbatch, dseq), state(z_fox, 2 * mix, mix, tp, dbatch, dseq),
              logf_s[None], state(z_dqk, mix, mix, tp, dbatch, dseq), state(z_dv, 0, mix, tp, dbatch, dseq))
    return (y_p.reshape(batch, seq, d), y_s.reshape(dbatch, dseq, d)) + outs_p + outs_s
```

```python
import functools
import math

import jax
import jax.numpy as jnp
from jax import lax
from jax.experimental import pallas as pl
from jax.experimental.pallas import tpu as pltpu

F32 = jnp.float32
BF16 = jnp.bfloat16

CHUNK = 64
ROPE_THETA = 500000.0
TOP_K = 4
SWIGLU_LIMIT = 7.0
SWIGLU_ALPHA = 1.702
MOE_BLOCK = 256
CHUNK_BLOCKS = 6
DMA_ISSUE_UNROLL = 8
NORM_ROWS = 16
NORM_EPS = 1e-5
LANE = 128
HEAD_W = 128
VMEM_LIMIT = 56 * 1024 * 1024
NEG_INIT = -1e30
ATTN_STRIP = 256


def _tile(n, pref):
    if n <= LANE:
        return n
    t = min(pref, n) // LANE * LANE
    while n % t:
        t -= LANE
    return t


def _tile_rows(n, pref):
    t = min(pref, n) // 8 * 8
    while n % t:
        t -= 8
    return t


def _params(sem):
    return pltpu.CompilerParams(dimension_semantics=sem, vmem_limit_bytes=VMEM_LIMIT)


def _rmsnorm_rows(x, g):
    ms = jnp.mean(x * x, axis=-1, keepdims=True)
    return x * lax.rsqrt(ms + NORM_EPS) * g


def _norm_pair_kernel(xp_ref, xs_ref, g_ref, o_ref, *, n_prompt_tiles):
    i = pl.program_id(0)

    @pl.when(i < n_prompt_tiles)
    def _():
        o_ref[...] = _rmsnorm_rows(xp_ref[...], g_ref[...]).astype(o_ref.dtype)

    @pl.when(i >= n_prompt_tiles)
    def _():
        o_ref[...] = _rmsnorm_rows(xs_ref[...], g_ref[...]).astype(o_ref.dtype)


def _norm_pair(xp, xs, g, tm):
    tp, d = xp.shape
    ts = xs.shape[0]
    npt, nst = tp // tm, ts // tm
    return pl.pallas_call(
        functools.partial(_norm_pair_kernel, n_prompt_tiles=npt),
        grid=(npt + nst,),
        in_specs=[
            pl.BlockSpec((tm, d), lambda i: (jnp.minimum(i, npt - 1), 0)),
            pl.BlockSpec((tm, d), lambda i: (jnp.maximum(i - npt, 0), 0)),
            pl.BlockSpec((1, d), lambda i: (0, 0)),
        ],
        out_specs=pl.BlockSpec((tm, d), lambda i: (i, 0)),
        out_shape=jax.ShapeDtypeStruct((tp + ts, d), BF16),
        compiler_params=_params(("arbitrary",)),
        name="norm_mix",
    )(xp, xs, g.reshape(1, d))


def _proj_kernel(a_ref, w_ref, *rest, mode):
    z = jnp.dot(a_ref[...], w_ref[...], preferred_element_type=F32)
    if mode == "plain":
        (o_ref,) = rest
        o_ref[...] = z
    elif mode == "sigmoid":
        (o_ref,) = rest
        o_ref[...] = jax.nn.sigmoid(z)
    elif mode == "logsig":
        b_ref, o_ref = rest
        o_ref[...] = jax.nn.log_sigmoid(z + b_ref[...])
    else:
        c_ref, s_ref, o_ref = rest
        tm, tn = z.shape
        lane = lax.broadcasted_iota(jnp.int32, (tm, LANE), 1)
        first_half = (lane % 64) < 8
        cos, sin = c_ref[...], s_ref[...]
        for c in range(tn // LANE):
            zc = z[:, c * LANE:(c + 1) * LANE]
            rot = jnp.where(first_half, pltpu.roll(zc, LANE - 8, 1), pltpu.roll(zc, 8, 1))
            o_ref[:, c * LANE:(c + 1) * LANE] = zc * cos + rot * sin


def _proj(a, w, n_cols, mode, extra=(), tm=1024, tn=512, name="proj"):
    t, d = a.shape
    tm = _tile(t, tm)
    tn = _tile(n_cols, tn)
    in_specs = [
        pl.BlockSpec((tm, d), lambda i, j: (i, 0)),
        pl.BlockSpec((d, tn), lambda i, j: (0, j)),
    ]
    if mode == "logsig":
        in_specs.append(pl.BlockSpec((1, tn), lambda i, j: (0, j)))
    elif mode == "rope":
        in_specs += [pl.BlockSpec((tm, LANE), lambda i, j: (i, 0))] * 2
    return pl.pallas_call(
        functools.partial(_proj_kernel, mode=mode),
        grid=(t // tm, n_cols // tn),
        in_specs=in_specs,
        out_specs=pl.BlockSpec((tm, tn), lambda i, j: (i, j)),
        out_shape=jax.ShapeDtypeStruct((t, n_cols), F32),
        compiler_params=_params(("arbitrary", "arbitrary")),
        name=name,
    )(a, w, *extra)


def _cumsum_kernel(x_ref, o_ref):
    rows, w = x_ref.shape
    lane = lax.broadcasted_iota(jnp.int32, (rows, LANE), 1)
    total = jnp.zeros((rows, 1), F32)
    for c in range(w // LANE):
        v = x_ref[:, c * LANE:(c + 1) * LANE]
        shift = 1
        while shift < LANE:
            v = v + jnp.where(lane >= shift, pltpu.roll(v, shift, 1), 0.0)
            shift *= 2
        v = v + total
        o_ref[:, c * LANE:(c + 1) * LANE] = v
        total = v[:, LANE - 1:LANE]


def _cumsum_seq(x):
    b, s, h = x.shape
    w = -(-s // LANE) * LANE
    xt = jnp.pad(x.transpose(0, 2, 1), ((0, 0), (0, 0), (0, w - s)))
    out = pl.pallas_call(
        _cumsum_kernel,
        grid=(b,),
        in_specs=[pl.BlockSpec((None, h, w), lambda i: (i, 0, 0))],
        out_specs=pl.BlockSpec((None, h, w), lambda i: (i, 0, 0)),
        out_shape=jax.ShapeDtypeStruct((b, h, w), F32),
        compiler_params=_params(("arbitrary",)),
        name="cumsum_logf",
    )(xt)
    return out[:, :, :s]


def _qk(q, k):
    return lax.dot_general(q, k, (((1,), (1,)), ((), ())), preferred_element_type=F32)


def _pick_lane(x, idx):
    lane = lax.broadcasted_iota(jnp.int32, x.shape, 1)
    return jnp.sum(jnp.where(lane == idx, x, 0.0), axis=1, keepdims=True)


def _diff_finish(o1, o2, lam, g, post_scale):
    o = o1 - lam * o2
    return _rmsnorm_rows(o, g) * post_scale


def _prompt_attn_kernel(*refs, mode, tq, rs, hb, scale, post_scale):
    if mode == "fox":
        q_ref, k_ref, v_ref, cq_ref, ck_ref, o_ref, q_s, kT_s, v_s, m_s, acc_s = refs
    else:
        lam_ref, q_ref, k_ref, v_ref, g_ref, o_ref, q_s, kT_s, v_s, m_s, acc_s = refs
    hg = pl.program_id(1)
    qi = pl.program_id(2)
    nk = kT_s.shape[1]
    seq = v_s.shape[1]
    maps_per_head = 1 if mode == "fox" else 2

    @pl.when(qi == 0)
    def _():
        for hh in range(hb):
            sl = slice(hh * HEAD_W, (hh + 1) * HEAD_W)
            v_s[hh, :, :HEAD_W] = v_ref[:, sl].astype(BF16)
            v_s[hh, :, HEAD_W:] = jnp.ones((seq, HEAD_W), BF16)
            for kj in range(nk):
                kT_s[hh, kj] = k_ref[kj * tq:(kj + 1) * tq, sl].T.astype(BF16)

    m_s[...] = jnp.full(m_s.shape, NEG_INIT, F32)
    acc_s[...] = jnp.zeros(acc_s.shape, F32)

    cqs = []
    for hh in range(hb):
        q = q_ref[:, hh * HEAD_W:(hh + 1) * HEAD_W] * scale
        if mode == "fox":
            q_s[hh] = q.astype(BF16)
            cqs.append(_pick_lane(cq_ref[...], hg * hb + hh))
        else:
            lane = lax.broadcasted_iota(jnp.int32, q.shape, 1)
            q_s[2 * hh] = jnp.where(lane < HEAD_W // 2, q, 0.0).astype(BF16)
            q_s[2 * hh + 1] = jnp.where(lane >= HEAD_W // 2, q, 0.0).astype(BF16)

    def tile(kj, diagonal):
        ks = pl.multiple_of(kj * tq, tq)
        for r0 in range(0, tq, rs):
            ncol = r0 + rs if diagonal else tq
            if diagonal:
                r = r0 + lax.broadcasted_iota(jnp.int32, (rs, ncol), 0)
                c = lax.broadcasted_iota(jnp.int32, (rs, ncol), 1)
                visible = (c <= r) if mode == "fox" else ((c // CHUNK) <= (r // CHUNK))
            for mi in range(hb * maps_per_head):
                hh = mi // maps_per_head
                s = jnp.dot(q_s[mi, r0:r0 + rs, :], kT_s[hh, kj, :, :ncol], preferred_element_type=F32)
                if mode == "fox":
                    ck = ck_ref[kj, pl.ds(hg * hb + hh, 1), :]
                    s = s + (cqs[hh][r0:r0 + rs] - ck[:, :ncol])
                if diagonal:
                    s = jnp.where(visible, s, -jnp.inf)
                m_prev = m_s[mi, r0:r0 + rs, :]
                m_new = jnp.maximum(m_prev, jnp.max(s, axis=-1, keepdims=True))
                alpha = jnp.exp(m_prev - m_new)
                p = jnp.exp(s - m_new[:, :1])
                pv = jnp.dot(p.astype(BF16), v_s[hh, pl.ds(ks, ncol), :], preferred_element_type=F32)
                acc_s[mi, r0:r0 + rs, :] = (jnp.concatenate([alpha, alpha], axis=1) * acc_s[mi, r0:r0 + rs, :]
                                            + pv)
                m_s[mi, r0:r0 + rs, :] = m_new

    def body(kj, carry):
        tile(kj, False)
        return carry

    lax.fori_loop(0, qi, body, 0)
    tile(qi, True)

    for hh in range(hb):
        sl = slice(hh * HEAD_W, (hh + 1) * HEAD_W)
        if mode == "fox":
            a = acc_s[hh]
            o = a[:, :HEAD_W] / a[:, HEAD_W:]
        else:
            a1, a2 = acc_s[2 * hh], acc_s[2 * hh + 1]
            o = _diff_finish(a1[:, :HEAD_W] / a1[:, HEAD_W:], a2[:, :HEAD_W] / a2[:, HEAD_W:],
                             lam_ref[0], g_ref[...], post_scale)
        o_ref[:, sl] = o.astype(o_ref.dtype)


def _prompt_attention(mode, q_arr, q_col0, k_arr, k_col0, v_arr, v_col0, batch, seq, n_heads,
                      extra, scale, post_scale=None):
    tq = _tile(seq, 512)
    nq = seq // tq
    hb = 2 if n_heads % 2 == 0 else 1
    w = hb * HEAD_W
    n_maps = hb if mode == "fox" else 2 * hb
    row_q = lambda b, hg, qi: (b * nq + qi, q_col0 // hb + hg)
    kv_spec = lambda col0: pl.BlockSpec((seq, w), lambda b, hg, qi: (b, col0 // hb + hg))
    if mode == "fox":
        cq, ck = extra
        in_specs = [
            pl.BlockSpec((tq, w), row_q), kv_spec(k_col0), kv_spec(v_col0),
            pl.BlockSpec((tq, n_heads), lambda b, hg, qi: (b * nq + qi, 0)),
            pl.BlockSpec((None, nq, n_heads, tq), lambda b, hg, qi: (b, 0, 0, 0)),
        ]
        args = (q_arr, k_arr, v_arr, cq, ck)
    else:
        lam, g = extra
        in_specs = [
            pl.BlockSpec(memory_space=pltpu.SMEM),
            pl.BlockSpec((tq, w), row_q), kv_spec(k_col0), kv_spec(v_col0),
            pl.BlockSpec((1, HEAD_W), lambda b, hg, qi: (0, 0)),
        ]
        args = (lam, q_arr, k_arr, v_arr, g)
    return pl.pallas_call(
        functools.partial(_prompt_attn_kernel, mode=mode, tq=tq, rs=min(ATTN_STRIP, tq), hb=hb,
                          scale=scale, post_scale=post_scale),
        grid=(batch, n_heads // hb, nq),
        in_specs=in_specs,
        out_specs=pl.BlockSpec((tq, w), lambda b, hg, qi: (b * nq + qi, hg)),
        out_shape=jax.ShapeDtypeStruct((batch * seq, n_heads * HEAD_W), BF16),
        scratch_shapes=[
            pltpu.VMEM((n_maps, tq, HEAD_W), BF16),
            pltpu.VMEM((hb, nq, HEAD_W, tq), BF16),
            pltpu.VMEM((hb, seq, 2 * HEAD_W), BF16),
            pltpu.VMEM((n_maps, tq, LANE), F32),
            pltpu.VMEM((n_maps, tq, 2 * HEAD_W), F32),
        ],
        compiler_params=_params(("arbitrary", "arbitrary", "arbitrary")),
        name=mode + "_prompt_attn",
    )(*args)


def _cache_fetch(kc_hbm, vc_hbm, kbuf, vbuf, sem, heads_per_step, n_groups, n_steps):
    n = pl.program_id(0) * n_groups + pl.program_id(1)

    def copies(step, slot):
        b = step // n_groups
        h0 = (step % n_groups) * heads_per_step
        cps = []
        for hh in range(heads_per_step):
            cps.append(pltpu.make_async_copy(kc_hbm.at[b, :, h0 + hh, :], kbuf.at[slot, hh], sem.at[slot]))
            cps.append(pltpu.make_async_copy(vc_hbm.at[b, :, h0 + hh, :], vbuf.at[slot, hh], sem.at[slot]))
        return cps

    @pl.when(n == 0)
    def _():
        for cp in copies(0, 0):
            cp.start()

    @pl.when(n + 1 < n_steps)
    def _():
        for cp in copies(n + 1, (n + 1) % 2):
            cp.start()

    slot = n % 2
    for cp in copies(n, slot):
        cp.wait()
    return slot


def _fox_sample_kernel(q_ref, kc_hbm, vc_hbm, kn_ref, vn_ref, cq_ref, ckc_ref, ckn_ref, o_ref,
                       kbuf, vbuf, sem, *, heads_per_step, n_groups, n_steps, scale):
    hg = pl.program_id(1)
    slot = _cache_fetch(kc_hbm, vc_hbm, kbuf, vbuf, sem, heads_per_step, n_groups, n_steps)
    for hh in range(heads_per_step):
        sl = slice(hh * HEAD_W, (hh + 1) * HEAD_W)
        h = hg * heads_per_step + hh
        q = (q_ref[:, sl] * scale).astype(BF16)
        cq = _pick_lane(cq_ref[...], h)
        sc = _qk(q, kbuf[slot, hh].astype(BF16)) + cq - ckc_ref[pl.ds(h, 1), :]
        sn = _qk(q, kn_ref[:, sl].astype(BF16)) + cq - ckn_ref[pl.ds(h, 1), :]
        r = lax.broadcasted_iota(jnp.int32, sn.shape, 0)
        c = lax.broadcasted_iota(jnp.int32, sn.shape, 1)
        sn = jnp.where(c <= r, sn, -jnp.inf)
        m = jnp.maximum(jnp.max(sc, axis=-1, keepdims=True), jnp.max(sn, axis=-1, keepdims=True))
        pc = jnp.exp(sc - m)
        pn = jnp.exp(sn - m)
        l = jnp.sum(pc, axis=-1, keepdims=True) + jnp.sum(pn, axis=-1, keepdims=True)
        o = (jnp.dot(pc.astype(BF16), vbuf[slot, hh].astype(BF16), preferred_element_type=F32)
             + jnp.dot(pn.astype(BF16), vn_ref[:, sl].astype(BF16), preferred_element_type=F32))
        o_ref[:, sl] = (o / l).astype(o_ref.dtype)


def _diff_sample_kernel(lam_ref, q_ref, kc_hbm, vc_hbm, kn_ref, vn_ref, g_ref, o_ref,
                        kbuf, vbuf, sem, *, heads_per_step, n_groups, n_steps, scale, post_scale):
    slot = _cache_fetch(kc_hbm, vc_hbm, kbuf, vbuf, sem, heads_per_step, n_groups, n_steps)
    for hh in range(heads_per_step):
        sl = slice(hh * HEAD_W, (hh + 1) * HEAD_W)
        q = q_ref[:, sl] * scale
        lane = lax.broadcasted_iota(jnp.int32, q.shape, 1)
        kc = kbuf[slot, hh].astype(BF16)
        kn = kn_ref[:, sl].astype(BF16)
        vc = vbuf[slot, hh].astype(BF16)
        vn = vn_ref[:, sl].astype(BF16)
        outs = []
        for keep in (lane < HEAD_W // 2, lane >= HEAD_W // 2):
            qh = jnp.where(keep, q, 0.0).astype(BF16)
            sc = _qk(qh, kc)
            sn = _qk(qh, kn)
            m = jnp.maximum(jnp.max(sc, axis=-1, keepdims=True), jnp.max(sn, axis=-1, keepdims=True))
            pc = jnp.exp(sc - m)
            pn = jnp.exp(sn - m)
            l = jnp.sum(pc, axis=-1, keepdims=True) + jnp.sum(pn, axis=-1, keepdims=True)
            o = (jnp.dot(pc.astype(BF16), vc, preferred_element_type=F32)
                 + jnp.dot(pn.astype(BF16), vn, preferred_element_type=F32))
            outs.append(o / l)
        o = _diff_finish(outs[0], outs[1], lam_ref[0], g_ref[...], post_scale)
        o_ref[:, sl] = o.astype(o_ref.dtype)


def _sample_attention(mode, q_arr, q_col0, kc_arr, vc_arr, kn_arr, kn_col0, vn_arr, vn_col0,
                      row0, batch, seq, past, n_heads, extra, scale, post_scale=None):
    hb = 4 if n_heads % 4 == 0 else 1
    w = hb * HEAD_W
    rb0 = row0 // seq
    n_groups = n_heads // hb
    ring = dict(heads_per_step=hb, n_groups=n_groups, n_steps=batch * n_groups)
    new = lambda col0: pl.BlockSpec((seq, w), lambda b, hg: (rb0 + b, col0 // hb + hg))
    cache = pl.BlockSpec(memory_space=pl.ANY)
    if mode == "fox":
        cq, ckc, ckn = extra
        kern = functools.partial(_fox_sample_kernel, scale=scale, **ring)
        in_specs = [
            new(q_col0), cache, cache, new(kn_col0), new(vn_col0),
            pl.BlockSpec((seq, n_heads), lambda b, hg: (b, 0)),
            pl.BlockSpec((None, n_heads, past), lambda b, hg: (b, 0, 0)),
            pl.BlockSpec((None, n_heads, seq), lambda b, hg: (b, 0, 0)),
        ]
        args = (q_arr, kc_arr, vc_arr, kn_arr, vn_arr, cq, ckc, ckn)
    else:
        lam, g = extra
        kern = functools.partial(_diff_sample_kernel, scale=scale, post_scale=post_scale, **ring)
        in_specs = [
            pl.BlockSpec(memory_space=pltpu.SMEM),
            new(q_col0), cache, cache, new(kn_col0), new(vn_col0),
            pl.BlockSpec((1, HEAD_W), lambda b, hg: (0, 0)),
        ]
        args = (lam, q_arr, kc_arr, vc_arr, kn_arr, vn_arr, g)
    return pl.pallas_call(
        kern,
        grid=(batch, n_groups),
        in_specs=in_specs,
        out_specs=pl.BlockSpec((seq, w), lambda b, hg: (b, hg)),
        out_shape=jax.ShapeDtypeStruct((batch * seq, n_heads * HEAD_W), BF16),
        scratch_shapes=[pltpu.VMEM((2, hb, past, HEAD_W), F32), pltpu.VMEM((2, hb, past, HEAD_W), F32),
                        pltpu.SemaphoreType.DMA((2,))],
        compiler_params=_params(("arbitrary", "arbitrary")),
        name=mode + "_sample_attn",
    )(*args)


def _merge_kernel(of_ref, od_ref, wf_ref, wd_ref, gf_ref, gd_ref, o_ref):
    a = jnp.dot(of_ref[...], wf_ref[...], preferred_element_type=F32)
    b = jnp.dot(od_ref[...], wd_ref[...], preferred_element_type=F32)
    o_ref[...] = (gf_ref[...] * a + gd_ref[...] * b).astype(o_ref.dtype)


def _merge(o_fox, o_diff, wb_fox, wb_diff, gates, tm=1024, tn=512):
    t, mix = o_fox.shape
    d = wb_fox.shape[1]
    tm, tn = _tile(t, tm), _tile(d, tn)
    nj = d // tn
    return pl.pallas_call(
        _merge_kernel,
        grid=(t // tm, nj),
        in_specs=[
            pl.BlockSpec((tm, mix), lambda i, j: (i, 0)),
            pl.BlockSpec((tm, mix), lambda i, j: (i, 0)),
            pl.BlockSpec((mix, tn), lambda i, j: (0, j)),
            pl.BlockSpec((mix, tn), lambda i, j: (0, j)),
            pl.BlockSpec((tm, tn), lambda i, j: (i, j)),
            pl.BlockSpec((tm, tn), lambda i, j: (i, nj + j)),
        ],
        out_specs=pl.BlockSpec((tm, tn), lambda i, j: (i, j)),
        out_shape=jax.ShapeDtypeStruct((t, d), BF16),
        compiler_params=_params(("arbitrary", "arbitrary")),
        name="branch_merge",
    )(o_fox, o_diff, wb_fox, wb_diff, gates, gates)


def _out_proj_kernel(a_ref, w_ref, xp_ref, xs_ref, o_ref, *, n_prompt_tiles):
    i = pl.program_id(0)
    z = jnp.dot(a_ref[...], w_ref[...], preferred_element_type=F32)

    @pl.when(i < n_prompt_tiles)
    def _():
        o_ref[...] = xp_ref[...] + z

    @pl.when(i >= n_prompt_tiles)
    def _():
        o_ref[...] = xs_ref[...] + z


def _out_proj(merged, w_out, xp, xs, tm=1024, tn=512):
    t, d = merged.shape
    tp, ts = xp.shape[0], xs.shape[0]
    tm = _tile(math.gcd(tp, ts), tm)
    tn = _tile(d, tn)
    npt = tp // tm
    return pl.pallas_call(
        functools.partial(_out_proj_kernel, n_prompt_tiles=npt),
        grid=(t // tm, d // tn),
        in_specs=[
            pl.BlockSpec((tm, d), lambda i, j: (i, 0)),
            pl.BlockSpec((d, tn), lambda i, j: (0, j)),
            pl.BlockSpec((tm, tn), lambda i, j: (jnp.minimum(i, npt - 1), j)),
            pl.BlockSpec((tm, tn), lambda i, j: (jnp.maximum(i - npt, 0), j)),
        ],
        out_specs=pl.BlockSpec((tm, tn), lambda i, j: (i, j)),
        out_shape=jax.ShapeDtypeStruct((t, d), F32),
        compiler_params=_params(("arbitrary", "arbitrary")),
        name="out_proj",
    )(merged, w_out, xp, xs)


def _router_kernel(h_ref, g_ref, w_ref, b_ref, o_ref):
    hn = _rmsnorm_rows(h_ref[...], g_ref[...])
    o_ref[...] = jnp.dot(hn, w_ref[...], preferred_element_type=F32,
                         precision=lax.Precision.HIGHEST) + b_ref[...]


def _router(h1, g, w_pad, b_pad, tm=256):
    t, d = h1.shape
    tm = _tile(t, tm)
    n = w_pad.shape[1]
    return pl.pallas_call(
        _router_kernel,
        grid=(t // tm,),
        in_specs=[
            pl.BlockSpec((tm, d), lambda i: (i, 0)),
            pl.BlockSpec((1, d), lambda i: (0, 0)),
            pl.BlockSpec((d, n), lambda i: (0, 0)),
            pl.BlockSpec((1, n), lambda i: (0, 0)),
        ],
        out_specs=pl.BlockSpec((tm, n), lambda i: (i, 0)),
        out_shape=jax.ShapeDtypeStruct((t, n), F32),
        compiler_params=_params(("arbitrary",)),
        name="router",
    )(h1, g.reshape(1, d), w_pad, b_pad)


def _gather_kernel(tok_ref, nu_ref, h_hbm, g_ref, o_ref, buf, sem):
    i = pl.program_id(0)
    n_used = nu_ref[0]

    def row_copy(tok, slot, r):
        return pltpu.make_async_copy(h_hbm.at[pl.ds(tok, 1), :], buf.at[slot, pl.ds(r, 1), :],
                                     sem.at[slot])

    def start_block(blk, slot):
        def body(r, carry):
            row_copy(tok_ref[blk * MOE_BLOCK + r], slot, r).start()
            return carry
        lax.fori_loop(0, MOE_BLOCK, body, 0, unroll=DMA_ISSUE_UNROLL)

    def wait_block(slot):
        def body(r, carry):
            row_copy(0, slot, r).wait()
            return carry
        lax.fori_loop(0, MOE_BLOCK, body, 0, unroll=DMA_ISSUE_UNROLL)

    @pl.when(i == 0)
    def _():
        start_block(0, 0)

    @pl.when(i + 1 < n_used)
    def _():
        start_block(i + 1, (i + 1) % 2)

    @pl.when(i < n_used)
    def _():
        slot = i % 2
        wait_block(slot)

        def norm_rows(c, carry):
            r0 = pl.multiple_of(c * NORM_ROWS, NORM_ROWS)
            x = buf[slot, pl.ds(r0, NORM_ROWS), :]
            o_ref[pl.ds(r0, NORM_ROWS), :] = _rmsnorm_rows(x, g_ref[...]).astype(o_ref.dtype)
            return carry
        lax.fori_loop(0, MOE_BLOCK // NORM_ROWS, norm_rows, 0, unroll=4)

    @pl.when(i >= n_used)
    def _():
        o_ref[...] = jnp.zeros(o_ref.shape, o_ref.dtype)


def _gather_norm(h1, g, row_tok, n_used, n_blocks):
    d = h1.shape[1]
    return pl.pallas_call(
        _gather_kernel,
        grid_spec=pltpu.PrefetchScalarGridSpec(
            num_scalar_prefetch=2,
            grid=(n_blocks,),
            in_specs=[
                pl.BlockSpec(memory_space=pl.ANY),
                pl.BlockSpec((1, d), lambda i, tok, nu: (0, 0)),
            ],
            out_specs=pl.BlockSpec((MOE_BLOCK, d), lambda i, tok, nu: (i, 0)),
            scratch_shapes=[pltpu.VMEM((2, MOE_BLOCK, d), F32), pltpu.SemaphoreType.DMA((2,))],
        ),
        out_shape=jax.ShapeDtypeStruct((n_blocks * MOE_BLOCK, d), BF16),
        compiler_params=_params(("arbitrary",)),
        name="moe_gather",
    )(row_tok, n_used, h1, g.reshape(1, d))


def _expert_chunk_kernel(ce_ref, cr_ref, cn_ref, meta_ref, *refs, mode, n_alloc_blocks, n_slots, nj):
    if mode == "up":
        a_hbm, wg_ref, wu_ref, bg_ref, bu_ref, o_hbm, abuf, w_s, obuf, isem, osem, zsem = refs
    else:
        a_hbm, rw_hbm, wd_ref, bd_ref, o_hbm, abuf, rwbuf, w_s, obuf, isem, osem, zsem = refs
    c = pl.program_id(0)
    j = pl.program_id(1)
    tn = obuf.shape[-1]
    mb = MOE_BLOCK
    n_chunks = meta_ref[0]
    n = c * nj + j
    last_active = n_chunks * nj - 1
    active = c < n_chunks
    nblk = cn_ref[c]
    r0 = cr_ref[c]

    def in_copies(b):
        rows_hbm = pl.ds((r0 + b) * mb, mb)
        rows_buf = pl.ds(b * mb, mb)
        cps = [pltpu.make_async_copy(a_hbm.at[rows_hbm, :], abuf.at[rows_buf, :], isem)]
        if mode == "down":
            cps.append(pltpu.make_async_copy(rw_hbm.at[rows_hbm, :], rwbuf.at[rows_buf, :], isem))
        return cps

    def out_copy(slot, row_blk, col, b):
        return pltpu.make_async_copy(
            obuf.at[slot, pl.ds(b * mb, mb), :],
            o_hbm.at[pl.ds((row_blk + b) * mb, mb), pl.ds(col * tn, tn)], osem.at[slot])

    def for_blocks(count, fn):
        for b in range(CHUNK_BLOCKS):
            pl.when(b < count)(functools.partial(fn, b))

    def start_in(b):
        for cp in in_copies(b):
            cp.start()

    def wait_in(b):
        for cp in in_copies(b):
            cp.wait()

    def wait_out(step, slot):
        cs = step // nj
        for_blocks(cn_ref[cs], lambda b: out_copy(slot, cr_ref[cs], step % nj, b).wait())

    @pl.when(active & (j == 0))
    def _():
        for_blocks(nblk, start_in)
        for_blocks(nblk, wait_in)

    @pl.when(active)
    def _():
        slot = n % 2

        @pl.when(n >= 2)
        def _():
            wait_out(n - 2, slot)

        if mode == "up":
            w_s[:, :tn] = wg_ref[...].astype(BF16)
            w_s[:, tn:] = wu_ref[...].astype(BF16)
            bias = jnp.concatenate([bg_ref[...], bu_ref[...]], axis=1)
        else:
            w_s[...] = wd_ref[...].astype(BF16)
            bias = bd_ref[...]

        def compute(row, m):
            rows = pl.ds(row, m)
            z = jnp.dot(abuf[rows, :], w_s[...], preferred_element_type=F32) + bias
            if mode == "up":
                g = jnp.minimum(z[:, :tn], SWIGLU_LIMIT)
                u = jnp.clip(z[:, tn:], -SWIGLU_LIMIT, SWIGLU_LIMIT)
                obuf[slot, rows, :] = ((u + 1.0) * (g * jax.nn.sigmoid(SWIGLU_ALPHA * g))).astype(obuf.dtype)
            else:
                rw = rwbuf[rows, :]
                for lc in range(tn // LANE):
                    obuf[slot, rows, lc * LANE:(lc + 1) * LANE] = z[:, lc * LANE:(lc + 1) * LANE] * rw

        for p in range(CHUNK_BLOCKS // 2):
            pl.when(2 * p + 1 < nblk)(functools.partial(compute, 2 * p * mb, 2 * mb))
            pl.when(2 * p + 1 == nblk)(functools.partial(compute, 2 * p * mb, mb))

        for_blocks(nblk, lambda b: out_copy(slot, r0, j, b).start())

        @pl.when(n == last_active)
        def _():
            @pl.when(n >= 1)
            def _():
                wait_out(n - 1, 1 - slot)
            wait_out(n, slot)

    @pl.when((c == n_slots - 1) & (j == nj - 1))
    def _():
        tail0 = meta_ref[1]
        n_tiles = (n_alloc_blocks - tail0) * nj
        obuf[0, 0:mb, :] = jnp.zeros((mb, tn), obuf.dtype)

        def zero_copy(t):
            return pltpu.make_async_copy(
                obuf.at[0, pl.ds(0, mb), :],
                o_hbm.at[pl.ds((tail0 + t // nj) * mb, mb), pl.ds((t % nj) * tn, tn)], zsem)

        def start(t, carry):
            zero_copy(t).start()
            return carry

        def wait(t, carry):
            zero_copy(t).wait()
            return carry

        lax.fori_loop(0, n_tiles, start, 0)
        lax.fori_loop(0, n_tiles, wait, 0)


def _expert_matmul(mode, a, weights, biases, chunks, row_w=None, tn=256):
    chunk_e, chunk_r0, chunk_n, meta = chunks
    n_rows, kdim = a.shape
    n_e, _, ndim = weights[0].shape
    tn = _tile(ndim, tn)
    nj = ndim // tn
    n_slots = chunk_e.shape[0]
    cols = CHUNK_BLOCKS * MOE_BLOCK

    def wmap(c, j, ce, cr, cn, meta):
        return (ce[c], 0, jnp.where(c < meta[0], j, nj - 1))

    w_specs = [pl.BlockSpec((None, kdim, tn), wmap) for _ in weights]
    b_specs = [pl.BlockSpec((None, 1, tn), wmap) for _ in biases]
    any_spec = pl.BlockSpec(memory_space=pl.ANY)
    if mode == "up":
        in_specs = [any_spec] + w_specs + b_specs
        args = [a] + list(weights) + [b.reshape(n_e, 1, ndim) for b in biases]
        out_dtype = BF16
        scratch = [pltpu.VMEM((cols, kdim), BF16), pltpu.VMEM((kdim, 2 * tn), BF16)]
    else:
        in_specs = [any_spec, any_spec] + w_specs + b_specs
        args = [a, row_w] + list(weights) + [b.reshape(n_e, 1, ndim) for b in biases]
        out_dtype = F32
        scratch = [pltpu.VMEM((cols, kdim), BF16), pltpu.VMEM((cols, LANE), F32),
                   pltpu.VMEM((kdim, tn), BF16)]
    scratch += [pltpu.VMEM((2, cols, tn), out_dtype), pltpu.SemaphoreType.DMA(()),
                pltpu.SemaphoreType.DMA((2,)), pltpu.SemaphoreType.DMA(())]
    return pl.pallas_call(
        functools.partial(_expert_chunk_kernel, mode=mode, n_alloc_blocks=n_rows // MOE_BLOCK,
                          n_slots=n_slots, nj=nj),
        grid_spec=pltpu.PrefetchScalarGridSpec(
            num_scalar_prefetch=4,
            grid=(n_slots, nj),
            in_specs=in_specs,
            out_specs=any_spec,
            scratch_shapes=scratch,
        ),
        out_shape=jax.ShapeDtypeStruct((n_rows, ndim), out_dtype),
        compiler_params=_params(("arbitrary", "arbitrary")),
        name="moe_" + mode,
    )(chunk_e, chunk_r0, chunk_n, meta, *args)


def _combine_kernel(pos_ref, h_ref, y_hbm, g_ref, op_ref, os_ref, buf, sem, *, tt, n_prompt_tiles):
    i = pl.program_id(0)
    n = pl.num_programs(0)

    def row_copy(p, slot, k, t):
        return pltpu.make_async_copy(y_hbm.at[pl.ds(p, 1), :], buf.at[slot, k, pl.ds(t, 1), :],
                                     sem.at[slot])

    def start_tile(tile, slot):
        def body(t, carry):
            for k in range(TOP_K):
                row_copy(pos_ref[(tile * tt + t) * TOP_K + k], slot, k, t).start()
            return carry
        lax.fori_loop(0, tt, body, 0, unroll=DMA_ISSUE_UNROLL // TOP_K)

    def wait_tile(slot):
        def body(t, carry):
            for k in range(TOP_K):
                row_copy(0, slot, k, t).wait()
            return carry
        lax.fori_loop(0, tt, body, 0, unroll=DMA_ISSUE_UNROLL // TOP_K)

    @pl.when(i == 0)
    def _():
        start_tile(0, 0)

    @pl.when(i + 1 < n)
    def _():
        start_tile(i + 1, (i + 1) % 2)

    slot = i % 2
    wait_tile(slot)

    def finish(o_ref):
        def rows(c, carry):
            r0 = pl.multiple_of(c * 8, 8)
            acc = h_ref[pl.ds(r0, 8), :]
            for k in range(TOP_K):
                acc = acc + buf[slot, k, pl.ds(r0, 8), :]
            o_ref[pl.ds(r0, 8), :] = _rmsnorm_rows(acc, g_ref[...])
            return carry
        lax.fori_loop(0, tt // 8, rows, 0, unroll=4)

    @pl.when(i < n_prompt_tiles)
    def _():
        finish(op_ref)

    @pl.when(i >= n_prompt_tiles)
    def _():
        finish(os_ref)


def _combine(h1, y, pos, g, tp, tt=64):
    t, d = h1.shape
    ts = t - tp
    tt = _tile_rows(math.gcd(tp, ts), tt)
    npt = tp // tt
    return pl.pallas_call(
        functools.partial(_combine_kernel, tt=tt, n_prompt_tiles=npt),
        grid_spec=pltpu.PrefetchScalarGridSpec(
            num_scalar_prefetch=1,
            grid=(t // tt,),
            in_specs=[
                pl.BlockSpec((tt, d), lambda i, pos: (i, 0)),
                pl.BlockSpec(memory_space=pl.ANY),
                pl.BlockSpec((1, d), lambda i, pos: (0, 0)),
            ],
            out_specs=[
                pl.BlockSpec((tt, d), lambda i, pos: (jnp.minimum(i, npt - 1), 0)),
                pl.BlockSpec((tt, d), lambda i, pos: (jnp.maximum(i - npt, 0), 0)),
            ],
            scratch_shapes=[pltpu.VMEM((2, TOP_K, tt, d), F32), pltpu.SemaphoreType.DMA((2,))],
        ),
        out_shape=[jax.ShapeDtypeStruct((tp, d), F32), jax.ShapeDtypeStruct((ts, d), F32)],
        compiler_params=_params(("arbitrary",)),
        name="moe_combine",
    )(pos, h1, y, g.reshape(1, d))


def _rope_tables(pos):
    half = (HEAD_W // 2) // 8
    inv_freq = ROPE_THETA ** (-jnp.arange(half, dtype=F32) * (2.0 / (2 * half)))
    ang = pos.astype(F32)[:, None] * inv_freq[None, :]
    cos, sin = jnp.cos(ang), jnp.sin(ang)
    ones = jnp.ones((pos.shape[0], HEAD_W // 2 - 2 * half), F32)
    c64 = jnp.concatenate([cos, cos, ones], axis=1)
    s64 = jnp.concatenate([-sin, sin, 0.0 * ones], axis=1)
    return jnp.concatenate([c64, c64], axis=1), jnp.concatenate([s64, s64], axis=1)


def _routing(logits, n_experts, n_blocks):
    t = logits.shape[0]
    top_v, top_i = lax.top_k(logits, TOP_K)
    gate_w = jax.nn.softmax(top_v, axis=-1)
    flat_e = top_i.reshape(-1).astype(jnp.int32)
    flat_t = jnp.repeat(jnp.arange(t, dtype=jnp.int32), TOP_K)
    flat_w = gate_w.reshape(-1)
    order = jnp.argsort(flat_e).astype(jnp.int32)
    slot_of = jnp.argsort(order).astype(jnp.int32)
    experts = jnp.arange(n_experts, dtype=jnp.int32)
    counts = jnp.sum((flat_e[:, None] == experts[None, :]).astype(jnp.int32), axis=0)
    starts = jnp.cumsum(counts) - counts
    padded = (counts + MOE_BLOCK - 1) // MOE_BLOCK * MOE_BLOCK
    padded_end = jnp.cumsum(padded)
    padded_start = padded_end - padded
    n_rows = n_blocks * MOE_BLOCK
    rows = jnp.arange(n_rows, dtype=jnp.int32)
    row_e = jnp.minimum(jnp.sum((rows[:, None] >= padded_end[None, :]).astype(jnp.int32), axis=1),
                        n_experts - 1)
    offset = rows - padded_start[row_e]
    valid = offset < counts[row_e]
    assign = order[jnp.clip(starts[row_e] + offset, 0, t * TOP_K - 1)]
    row_tok = jnp.where(valid, flat_t[assign], 0)
    row_w = jnp.where(valid, flat_w[assign], 0.0)
    pos = padded_start[flat_e] + slot_of - starts[flat_e]
    n_used = (padded_end[-1] // MOE_BLOCK).astype(jnp.int32)
    blocks = padded // MOE_BLOCK
    per_expert = (blocks + CHUNK_BLOCKS - 1) // CHUNK_BLOCKS
    chunk_end = jnp.cumsum(per_expert)
    n_chunks = chunk_end[-1]
    n_slots = -(-n_blocks // CHUNK_BLOCKS) + n_experts
    slots = jnp.arange(n_slots, dtype=jnp.int32)
    s_eff = jnp.minimum(slots, n_chunks - 1)
    chunk_e = jnp.sum((s_eff[:, None] >= chunk_end[None, :]).astype(jnp.int32), axis=1)
    k = s_eff - (chunk_end - per_expert)[chunk_e]
    real = slots < n_chunks
    chunk_r0 = jnp.where(real, padded_start[chunk_e] // MOE_BLOCK + k * CHUNK_BLOCKS, 0)
    chunk_n = jnp.where(real, jnp.clip(blocks[chunk_e] - k * CHUNK_BLOCKS, 0, CHUNK_BLOCKS), 0)
    meta = jnp.stack([n_chunks, n_used]).astype(jnp.int32)
    chunks = (chunk_e.astype(jnp.int32), chunk_r0.astype(jnp.int32), chunk_n.astype(jnp.int32), meta)
    return row_tok, row_w, pos, chunks, n_used.reshape(1)


def kernel(x_prompt, x_sample, cache_fox_k, cache_fox_v, cache_fox_logf, cache_diff_k, cache_diff_v, norm_mix, w_in, b_forget, lambda_q1, lambda_k1, lambda_q2, lambda_k2, diff_subln, w_branch, w_out, norm_ffn, w_router, b_router, moe_w_gate, moe_b_gate, moe_w_up, moe_b_up, moe_w_down, moe_b_down, norm_final):
    batch, seq, d = x_prompt.shape
    dbatch, dseq, _ = x_sample.shape
    depth, _, past, n_heads, _ = cache_fox_k.shape
    assert depth == 1 and cache_fox_k.shape[-1] == HEAD_W and cache_diff_k.shape[-1] == HEAD_W
    assert dseq == CHUNK and past % CHUNK == 0, "sample queries must form exactly the newest chunk"
    mix = n_heads * HEAD_W
    n_experts = w_router.shape[-1]
    tp, ts = batch * seq, dbatch * dseq
    t = tp + ts
    assert (t * TOP_K) % MOE_BLOCK == 0
    xp = x_prompt.reshape(tp, d)
    xs = x_sample.reshape(ts, d)

    off_f = 3 * mix
    off_dq = off_f + n_heads
    off_dv = off_dq + 2 * mix
    off_gate = off_dv + mix
    w = w_in[0]
    w_fox = w[:, :off_f].astype(BF16)
    w_fgt = jnp.pad(w[:, off_f:off_dq], ((0, 0), (0, LANE - n_heads))).astype(BF16)
    b_fgt = jnp.pad(b_forget[0], (0, LANE - n_heads)).reshape(1, LANE)
    w_dqk = w[:, off_dq:off_dv].astype(BF16)
    w_dv = w[:, off_dv:off_gate].astype(BF16)
    w_gates = w[:, off_gate:].astype(BF16)

    row_tile = _tile(math.gcd(tp, ts), 1024)
    xn = _norm_pair(xp, xs, norm_mix[0], _tile(math.gcd(tp, ts), 256))

    pos_all = jnp.concatenate([jnp.tile(jnp.arange(seq), batch), jnp.tile(past + jnp.arange(dseq), dbatch)])
    cos_t, sin_t = _rope_tables(pos_all)

    z_fox = _proj(xn, w_fox, 3 * mix, "plain", tm=row_tile, name="proj_fox")
    logf = _proj(xn, w_fgt, LANE, "logsig", extra=(b_fgt,), tm=row_tile, name="proj_forget")[:, :n_heads]
    z_dqk = _proj(xn, w_dqk, 2 * mix, "rope", extra=(cos_t, sin_t), tm=row_tile, name="proj_diff_qk")
    z_dv = _proj(xn, w_dv, mix, "plain", tm=row_tile, name="proj_diff_v")
    gates = _proj(xn, w_gates, 2 * d, "sigmoid", tm=row_tile, name="proj_gates")

    logf_p = logf[:tp].reshape(batch, seq, n_heads)
    logf_s = logf[tp:].reshape(dbatch, dseq, n_heads)
    ck_p_flat = _cumsum_seq(logf_p)
    ck_s = _cumsum_seq(jnp.concatenate([cache_fox_logf[0].astype(F32), logf_s], axis=1))
    cum_p = ck_p_flat.transpose(0, 2, 1)
    tq = _tile(seq, 512)
    ck_p = ck_p_flat.reshape(batch, n_heads, seq // tq, tq).transpose(0, 2, 1, 3)
    cq_s = ck_s[:, :, past:].transpose(0, 2, 1).reshape(ts, n_heads)

    f32 = F32
    lam_init = 0.8 - 0.6 * math.exp(-0.3 * 0)
    lam = (jnp.exp(jnp.sum(lambda_q1[0].astype(f32) * lambda_k1[0].astype(f32)))
           - jnp.exp(jnp.sum(lambda_q2[0].astype(f32) * lambda_k2[0].astype(f32))) + lam_init).reshape(1)
    subln = diff_subln[0].reshape(1, HEAD_W)
    nh = n_heads
    fox_scale = HEAD_W ** -0.5
    diff_scale = (HEAD_W // 2) ** -0.5

    o_fox_p = _prompt_attention("fox", z_fox, 0, z_fox, nh, z_fox, 2 * nh, batch, seq, nh,
                                (cum_p.reshape(tp, nh), ck_p), fox_scale)
    o_diff_p = _prompt_attention("diff", z_dqk, 0, z_dqk, nh, z_dv, 0, batch, seq, nh,
                                 (lam, subln), diff_scale, post_scale=1.0 - lam_init)
    o_fox_s = _sample_attention("fox", z_fox, 0, cache_fox_k[0], cache_fox_v[0], z_fox, nh, z_fox, 2 * nh,
                                tp, dbatch, dseq, past, nh,
                                (cq_s, ck_s[:, :, :past], ck_s[:, :, past:]), fox_scale)
    o_diff_s = _sample_attention("diff", z_dqk, 0, cache_diff_k[0], cache_diff_v[0], z_dqk, nh, z_dv, 0,
                                 tp, dbatch, dseq, past, nh, (lam, subln), diff_scale,
                                 post_scale=1.0 - lam_init)
    o_fox = jnp.concatenate([o_fox_p, o_fox_s], axis=0)
    o_diff = jnp.concatenate([o_diff_p, o_diff_s], axis=0)

    merged = _merge(o_fox, o_diff, w_branch[0, 0].astype(BF16), w_branch[0, 1].astype(BF16), gates,
                    tm=row_tile)
    h1 = _out_proj(merged, w_out[0].astype(BF16), xp, xs, tm=row_tile)

    w_r = jnp.pad(w_router[0], ((0, 0), (0, LANE - n_experts)))
    b_r = jnp.pad(b_router[0].astype(F32), (0, LANE - n_experts)).reshape(1, LANE)
    logits = _router(h1, norm_ffn[0], w_r, b_r)[:, :n_experts]
    n_blocks = (t * TOP_K) // MOE_BLOCK + n_experts
    row_tok, row_w, pos, chunks, n_used = _routing(logits, n_experts, n_blocks)
    xs_sorted = _gather_norm(h1, norm_ffn[0], row_tok, n_used, n_blocks)
    hmid = _expert_matmul("up", xs_sorted, [moe_w_gate[0], moe_w_up[0]], [moe_b_gate[0], moe_b_up[0]],
                          chunks, tn=256)
    row_w_b = jnp.broadcast_to(row_w[:, None], (row_w.shape[0], LANE))
    y_rows = _expert_matmul("down", hmid, [moe_w_down[0]], [moe_b_down[0]], chunks, row_w=row_w_b,
                            tn=512)
    y_p, y_s = _combine(h1, y_rows, pos, norm_final, tp)

    def state(arr, col0, width, rows0, b, s):
        return arr[rows0:rows0 + b * s, col0:col0 + width].reshape(1, b, s, n_heads, HEAD_W)

    outs_p = (state(z_fox, mix, mix, 0, batch, seq), state(z_fox, 2 * mix, mix, 0, batch, seq),
              logf_p[None], state(z_dqk, mix, mix, 0, batch, seq), state(z_dv, 0, mix, 0, batch, seq))
    outs_s = (state(z_fox, mix, mix, tp, dbatch, dseq), state(z_fox, 2 * mix, mix, tp, dbatch, dseq),
              logf_s[None], state(z_dqk, mix, mix, tp, dbatch, dseq), state(z_dv, 0, mix, tp, dbatch, dseq))
    return (y_p.reshape(batch, seq, d), y_s.reshape(dbatch, dseq, d)) + outs_p + outs_s
```

```python
import functools
import math

import jax
import jax.numpy as jnp
from jax import lax
from jax.experimental import pallas as pl
from jax.experimental.pallas import tpu as pltpu

F32 = jnp.float32
BF16 = jnp.bfloat16

CHUNK = 64
ROPE_THETA = 500000.0
TOP_K = 4
SWIGLU_LIMIT = 7.0
SWIGLU_ALPHA = 1.702
MOE_BLOCK = 256
CHUNK_BLOCKS = 6
DMA_ISSUE_UNROLL = 8
NORM_ROWS = 16
NORM_EPS = 1e-5
LANE = 128
HEAD_W = 128
VMEM_LIMIT = 56 * 1024 * 1024
NEG_INIT = -1e30
ATTN_STRIP = 256


def _tile(n, pref):
    if n <= LANE:
        return n
    t = min(pref, n) // LANE * LANE
    while n % t:
        t -= LANE
    return t


def _tile_rows(n, pref):
    t = min(pref, n) // 8 * 8
    while n % t:
        t -= 8
    return t


def _params(sem):
    return pltpu.CompilerParams(dimension_semantics=sem, vmem_limit_bytes=VMEM_LIMIT)


def _rmsnorm_rows(x, g):
    ms = jnp.mean(x * x, axis=-1, keepdims=True)
    return x * lax.rsqrt(ms + NORM_EPS) * g


def _norm_pair_kernel(xp_ref, xs_ref, g_ref, o_ref, *, n_prompt_tiles):
    i = pl.program_id(0)

    @pl.when(i < n_prompt_tiles)
    def _():
        o_ref[...] = _rmsnorm_rows(xp_ref[...], g_ref[...]).astype(o_ref.dtype)

    @pl.when(i >= n_prompt_tiles)
    def _():
        o_ref[...] = _rmsnorm_rows(xs_ref[...], g_ref[...]).astype(o_ref.dtype)


def _norm_pair(xp, xs, g, tm):
    tp, d = xp.shape
    ts = xs.shape[0]
    npt, nst = tp // tm, ts // tm
    return pl.pallas_call(
        functools.partial(_norm_pair_kernel, n_prompt_tiles=npt),
        grid=(npt + nst,),
        in_specs=[
            pl.BlockSpec((tm, d), lambda i: (jnp.minimum(i, npt - 1), 0)),
            pl.BlockSpec((tm, d), lambda i: (jnp.maximum(i - npt, 0), 0)),
            pl.BlockSpec((1, d), lambda i: (0, 0)),
        ],
        out_specs=pl.BlockSpec((tm, d), lambda i: (i, 0)),
        out_shape=jax.ShapeDtypeStruct((tp + ts, d), BF16),
        compiler_params=_params(("arbitrary",)),
        name="norm_mix",
    )(xp, xs, g.reshape(1, d))


def _proj_kernel(a_ref, w_ref, *rest, mode, n_prompt_tiles):
    n_out = 1 if n_prompt_tiles is None else 2
    extra, outs = rest[:len(rest) - n_out], rest[len(rest) - n_out:]
    z = jnp.dot(a_ref[...], w_ref[...], preferred_element_type=F32)

    def store(o_ref):
        if mode == "plain":
            o_ref[...] = z
        elif mode == "sigmoid":
            o_ref[...] = jax.nn.sigmoid(z)
        elif mode == "logsig":
            (b_ref,) = extra
            o_ref[...] = jax.nn.log_sigmoid(z + b_ref[...])
        else:
            c_ref, s_ref = extra
            tm, tn = z.shape
            lane = lax.broadcasted_iota(jnp.int32, (tm, LANE), 1)
            first_half = (lane % 64) < 8
            cos, sin = c_ref[...], s_ref[...]
            for c in range(tn // LANE):
                zc = z[:, c * LANE:(c + 1) * LANE]
                rot = jnp.where(first_half, pltpu.roll(zc, LANE - 8, 1), pltpu.roll(zc, 8, 1))
                o_ref[:, c * LANE:(c + 1) * LANE] = zc * cos + rot * sin

    if n_prompt_tiles is None:
        store(outs[0])
    else:
        i = pl.program_id(0)
        pl.when(i < n_prompt_tiles)(functools.partial(store, outs[0]))
        pl.when(i >= n_prompt_tiles)(functools.partial(store, outs[1]))


def _proj(a, w, n_cols, mode, extra=(), tm=1024, tn=512, name="proj", split=None):
    t, d = a.shape
    tm = _tile(t, tm)
    tn = _tile(n_cols, tn)
    nj = n_cols // tn
    in_specs = [
        pl.BlockSpec((tm, d), lambda i, j: (i, 0)),
        pl.BlockSpec((d, tn), lambda i, j: (0, j)),
    ]
    if mode == "logsig":
        in_specs.append(pl.BlockSpec((1, tn), lambda i, j: (0, j)))
    elif mode == "rope":
        in_specs += [pl.BlockSpec((tm, LANE), lambda i, j: (i, 0))] * 2
    if split is None:
        npt = None
        out_specs = pl.BlockSpec((tm, tn), lambda i, j: (i, j))
        out_shape = jax.ShapeDtypeStruct((t, n_cols), F32)
    else:
        assert split % tm == 0
        npt = split // tm
        out_specs = [
            pl.BlockSpec((tm, tn), lambda i, j: (jnp.minimum(i, npt - 1), jnp.where(i < npt, j, nj - 1))),
            pl.BlockSpec((tm, tn), lambda i, j: (jnp.maximum(i - npt, 0), jnp.where(i >= npt, j, 0))),
        ]
        out_shape = [jax.ShapeDtypeStruct((split, n_cols), F32),
                     jax.ShapeDtypeStruct((t - split, n_cols), F32)]
    return pl.pallas_call(
        functools.partial(_proj_kernel, mode=mode, n_prompt_tiles=npt),
        grid=(t // tm, nj),
        in_specs=in_specs,
        out_specs=out_specs,
        out_shape=out_shape,
        compiler_params=_params(("arbitrary", "arbitrary")),
        name=name,
    )(a, w, *extra)


def _cumsum_kernel(x_ref, o_ref):
    rows, w = x_ref.shape
    lane = lax.broadcasted_iota(jnp.int32, (rows, LANE), 1)
    total = jnp.zeros((rows, 1), F32)
    for c in range(w // LANE):
        v = x_ref[:, c * LANE:(c + 1) * LANE]
        shift = 1
        while shift < LANE:
            v = v + jnp.where(lane >= shift, pltpu.roll(v, shift, 1), 0.0)
            shift *= 2
        v = v + total
        o_ref[:, c * LANE:(c + 1) * LANE] = v
        total = v[:, LANE - 1:LANE]


def _cumsum_seq(x):
    b, s, h = x.shape
    w = -(-s // LANE) * LANE
    xt = jnp.pad(x.transpose(0, 2, 1), ((0, 0), (0, 0), (0, w - s)))
    out = pl.pallas_call(
        _cumsum_kernel,
        grid=(b,),
        in_specs=[pl.BlockSpec((None, h, w), lambda i: (i, 0, 0))],
        out_specs=pl.BlockSpec((None, h, w), lambda i: (i, 0, 0)),
        out_shape=jax.ShapeDtypeStruct((b, h, w), F32),
        compiler_params=_params(("arbitrary",)),
        name="cumsum_logf",
    )(xt)
    return out[:, :, :s]


def _qk(q, k):
    return lax.dot_general(q, k, (((1,), (1,)), ((), ())), preferred_element_type=F32)


def _pick_lane(x, idx):
    lane = lax.broadcasted_iota(jnp.int32, x.shape, 1)
    return jnp.sum(jnp.where(lane == idx, x, 0.0), axis=1, keepdims=True)


def _diff_finish(o1, o2, lam, g, post_scale):
    o = o1 - lam * o2
    return _rmsnorm_rows(o, g) * post_scale


def _prompt_attn_kernel(*refs, mode, tq, rs, hb, scale, post_scale):
    if mode == "fox":
        q_ref, k_ref, v_ref, cq_ref, ck_ref, o_ref, q_s, kT_s, v_s, m_s, acc_s = refs
    else:
        lam_ref, q_ref, k_ref, v_ref, g_ref, o_ref, q_s, kT_s, v_s, m_s, acc_s = refs
    hg = pl.program_id(1)
    qi = pl.program_id(2)
    nk = kT_s.shape[1]
    seq = v_s.shape[1]
    maps_per_head = 1 if mode == "fox" else 2

    @pl.when(qi == 0)
    def _():
        for hh in range(hb):
            sl = slice(hh * HEAD_W, (hh + 1) * HEAD_W)
            v_s[hh, :, :HEAD_W] = v_ref[:, sl].astype(BF16)
            v_s[hh, :, HEAD_W:] = jnp.ones((seq, HEAD_W), BF16)
            for kj in range(nk):
                kT_s[hh, kj] = k_ref[kj * tq:(kj + 1) * tq, sl].T.astype(BF16)

    m_s[...] = jnp.full(m_s.shape, NEG_INIT, F32)
    acc_s[...] = jnp.zeros(acc_s.shape, F32)

    cqs = []
    for hh in range(hb):
        q = q_ref[:, hh * HEAD_W:(hh + 1) * HEAD_W] * scale
        if mode == "fox":
            q_s[hh] = q.astype(BF16)
            cqs.append(_pick_lane(cq_ref[...], hg * hb + hh))
        else:
            lane = lax.broadcasted_iota(jnp.int32, q.shape, 1)
            q_s[2 * hh] = jnp.where(lane < HEAD_W // 2, q, 0.0).astype(BF16)
            q_s[2 * hh + 1] = jnp.where(lane >= HEAD_W // 2, q, 0.0).astype(BF16)

    def tile(kj, diagonal):
        ks = pl.multiple_of(kj * tq, tq)
        for r0 in range(0, tq, rs):
            ncol = r0 + rs if diagonal else tq
            if diagonal:
                r = r0 + lax.broadcasted_iota(jnp.int32, (rs, ncol), 0)
                c = lax.broadcasted_iota(jnp.int32, (rs, ncol), 1)
                visible = (c <= r) if mode == "fox" else ((c // CHUNK) <= (r // CHUNK))
            for mi in range(hb * maps_per_head):
                hh = mi // maps_per_head
                s = jnp.dot(q_s[mi, r0:r0 + rs, :], kT_s[hh, kj, :, :ncol], preferred_element_type=F32)
                if mode == "fox":
                    ck = ck_ref[kj, pl.ds(hg * hb + hh, 1), :]
                    s = s + (cqs[hh][r0:r0 + rs] - ck[:, :ncol])
                if diagonal:
                    s = jnp.where(visible, s, -jnp.inf)
                m_prev = m_s[mi, r0:r0 + rs, :]
                m_new = jnp.maximum(m_prev, jnp.max(s, axis=-1, keepdims=True))
                alpha = jnp.exp(m_prev - m_new)
                p = jnp.exp(s - m_new[:, :1])
                pv = jnp.dot(p.astype(BF16), v_s[hh, pl.ds(ks, ncol), :], preferred_element_type=F32)
                acc_s[mi, r0:r0 + rs, :] = (jnp.concatenate([alpha, alpha], axis=1) * acc_s[mi, r0:r0 + rs, :]
                                            + pv)
                m_s[mi, r0:r0 + rs, :] = m_new

    def body(kj, carry):
        tile(kj, False)
        return carry

    lax.fori_loop(0, qi, body, 0)
    tile(qi, True)

    for hh in range(hb):
        sl = slice(hh * HEAD_W, (hh + 1) * HEAD_W)
        if mode == "fox":
            a = acc_s[hh]
            o = a[:, :HEAD_W] / a[:, HEAD_W:]
        else:
            a1, a2 = acc_s[2 * hh], acc_s[2 * hh + 1]
            o = _diff_finish(a1[:, :HEAD_W] / a1[:, HEAD_W:], a2[:, :HEAD_W] / a2[:, HEAD_W:],
                             lam_ref[0], g_ref[...], post_scale)
        o_ref[:, sl] = o.astype(o_ref.dtype)


def _prompt_attention(mode, q_arr, q_col0, k_arr, k_col0, v_arr, v_col0, batch, seq, n_heads,
                      extra, scale, post_scale=None):
    tq = _tile(seq, 512)
    nq = seq // tq
    hb = 2 if n_heads % 2 == 0 else 1
    w = hb * HEAD_W
    n_maps = hb if mode == "fox" else 2 * hb
    row_q = lambda b, hg, qi: (b * nq + qi, q_col0 // hb + hg)
    kv_spec = lambda col0: pl.BlockSpec((seq, w), lambda b, hg, qi: (b, col0 // hb + hg))
    if mode == "fox":
        cq, ck = extra
        in_specs = [
            pl.BlockSpec((tq, w), row_q), kv_spec(k_col0), kv_spec(v_col0),
            pl.BlockSpec((tq, n_heads), lambda b, hg, qi: (b * nq + qi, 0)),
            pl.BlockSpec((None, nq, n_heads, tq), lambda b, hg, qi: (b, 0, 0, 0)),
        ]
        args = (q_arr, k_arr, v_arr, cq, ck)
    else:
        lam, g = extra
        in_specs = [
            pl.BlockSpec(memory_space=pltpu.SMEM),
            pl.BlockSpec((tq, w), row_q), kv_spec(k_col0), kv_spec(v_col0),
            pl.BlockSpec((1, HEAD_W), lambda b, hg, qi: (0, 0)),
        ]
        args = (lam, q_arr, k_arr, v_arr, g)
    return pl.pallas_call(
        functools.partial(_prompt_attn_kernel, mode=mode, tq=tq, rs=min(ATTN_STRIP, tq), hb=hb,
                          scale=scale, post_scale=post_scale),
        grid=(batch, n_heads // hb, nq),
        in_specs=in_specs,
        out_specs=pl.BlockSpec((tq, w), lambda b, hg, qi: (b * nq + qi, hg)),
        out_shape=jax.ShapeDtypeStruct((batch * seq, n_heads * HEAD_W), BF16),
        scratch_shapes=[
            pltpu.VMEM((n_maps, tq, HEAD_W), BF16),
            pltpu.VMEM((hb, nq, HEAD_W, tq), BF16),
            pltpu.VMEM((hb, seq, 2 * HEAD_W), BF16),
            pltpu.VMEM((n_maps, tq, LANE), F32),
            pltpu.VMEM((n_maps, tq, 2 * HEAD_W), F32),
        ],
        compiler_params=_params(("arbitrary", "arbitrary", "arbitrary")),
        name=mode + "_prompt_attn",
    )(*args)


def _cache_fetch(kc_hbm, vc_hbm, kbuf, vbuf, sem, heads_per_step, n_groups, n_steps):
    n = pl.program_id(0) * n_groups + pl.program_id(1)

    def copies(step, slot):
        b = step // n_groups
        h0 = (step % n_groups) * heads_per_step
        cps = []
        for hh in range(heads_per_step):
            cps.append(pltpu.make_async_copy(kc_hbm.at[b, :, h0 + hh, :], kbuf.at[slot, hh], sem.at[slot]))
            cps.append(pltpu.make_async_copy(vc_hbm.at[b, :, h0 + hh, :], vbuf.at[slot, hh], sem.at[slot]))
        return cps

    @pl.when(n == 0)
    def _():
        for cp in copies(0, 0):
            cp.start()

    @pl.when(n + 1 < n_steps)
    def _():
        for cp in copies(n + 1, (n + 1) % 2):
            cp.start()

    slot = n % 2
    for cp in copies(n, slot):
        cp.wait()
    return slot


def _fox_sample_kernel(q_ref, kc_hbm, vc_hbm, kn_ref, vn_ref, cq_ref, ckc_ref, ckn_ref, o_ref,
                       kbuf, vbuf, sem, *, heads_per_step, n_groups, n_steps, scale):
    hg = pl.program_id(1)
    slot = _cache_fetch(kc_hbm, vc_hbm, kbuf, vbuf, sem, heads_per_step, n_groups, n_steps)
    for hh in range(heads_per_step):
        sl = slice(hh * HEAD_W, (hh + 1) * HEAD_W)
        h = hg * heads_per_step + hh
        q = (q_ref[:, sl] * scale).astype(BF16)
        cq = _pick_lane(cq_ref[...], h)
        sc = _qk(q, kbuf[slot, hh].astype(BF16)) + cq - ckc_ref[pl.ds(h, 1), :]
        sn = _qk(q, kn_ref[:, sl].astype(BF16)) + cq - ckn_ref[pl.ds(h, 1), :]
        r = lax.broadcasted_iota(jnp.int32, sn.shape, 0)
        c = lax.broadcasted_iota(jnp.int32, sn.shape, 1)
        sn = jnp.where(c <= r, sn, -jnp.inf)
        m = jnp.maximum(jnp.max(sc, axis=-1, keepdims=True), jnp.max(sn, axis=-1, keepdims=True))
        pc = jnp.exp(sc - m)
        pn = jnp.exp(sn - m)
        l = jnp.sum(pc, axis=-1, keepdims=True) + jnp.sum(pn, axis=-1, keepdims=True)
        o = (jnp.dot(pc.astype(BF16), vbuf[slot, hh].astype(BF16), preferred_element_type=F32)
             + jnp.dot(pn.astype(BF16), vn_ref[:, sl].astype(BF16), preferred_element_type=F32))
        o_ref[:, sl] = (o / l).astype(o_ref.dtype)


def _diff_sample_kernel(lam_ref, q_ref, kc_hbm, vc_hbm, kn_ref, vn_ref, g_ref, o_ref,
                        kbuf, vbuf, sem, *, heads_per_step, n_groups, n_steps, scale, post_scale):
    slot = _cache_fetch(kc_hbm, vc_hbm, kbuf, vbuf, sem, heads_per_step, n_groups, n_steps)
    for hh in range(heads_per_step):
        sl = slice(hh * HEAD_W, (hh + 1) * HEAD_W)
        q = q_ref[:, sl] * scale
        lane = lax.broadcasted_iota(jnp.int32, q.shape, 1)
        kc = kbuf[slot, hh].astype(BF16)
        kn = kn_ref[:, sl].astype(BF16)
        vc = vbuf[slot, hh].astype(BF16)
        vn = vn_ref[:, sl].astype(BF16)
        outs = []
        for keep in (lane < HEAD_W // 2, lane >= HEAD_W // 2):
            qh = jnp.where(keep, q, 0.0).astype(BF16)
            sc = _qk(qh, kc)
            sn = _qk(qh, kn)
            m = jnp.maximum(jnp.max(sc, axis=-1, keepdims=True), jnp.max(sn, axis=-1, keepdims=True))
            pc = jnp.exp(sc - m)
            pn = jnp.exp(sn - m)
            l = jnp.sum(pc, axis=-1, keepdims=True) + jnp.sum(pn, axis=-1, keepdims=True)
            o = (jnp.dot(pc.astype(BF16), vc, preferred_element_type=F32)
                 + jnp.dot(pn.astype(BF16), vn, preferred_element_type=F32))
            outs.append(o / l)
        o = _diff_finish(outs[0], outs[1], lam_ref[0], g_ref[...], post_scale)
        o_ref[:, sl] = o.astype(o_ref.dtype)


def _sample_attention(mode, q_arr, q_col0, kc_arr, vc_arr, kn_arr, kn_col0, vn_arr, vn_col0,
                      row0, batch, seq, past, n_heads, extra, scale, post_scale=None):
    hb = 4 if n_heads % 4 == 0 else 1
    w = hb * HEAD_W
    rb0 = row0 // seq
    n_groups = n_heads // hb
    ring = dict(heads_per_step=hb, n_groups=n_groups, n_steps=batch * n_groups)
    new = lambda col0: pl.BlockSpec((seq, w), lambda b, hg: (rb0 + b, col0 // hb + hg))
    cache = pl.BlockSpec(memory_space=pl.ANY)
    if mode == "fox":
        cq, ckc, ckn = extra
        kern = functools.partial(_fox_sample_kernel, scale=scale, **ring)
        in_specs = [
            new(q_col0), cache, cache, new(kn_col0), new(vn_col0),
            pl.BlockSpec((seq, n_heads), lambda b, hg: (b, 0)),
            pl.BlockSpec((None, n_heads, past), lambda b, hg: (b, 0, 0)),
            pl.BlockSpec((None, n_heads, seq), lambda b, hg: (b, 0, 0)),
        ]
        args = (q_arr, kc_arr, vc_arr, kn_arr, vn_arr, cq, ckc, ckn)
    else:
        lam, g = extra
        kern = functools.partial(_diff_sample_kernel, scale=scale, post_scale=post_scale, **ring)
        in_specs = [
            pl.BlockSpec(memory_space=pltpu.SMEM),
            new(q_col0), cache, cache, new(kn_col0), new(vn_col0),
            pl.BlockSpec((1, HEAD_W), lambda b, hg: (0, 0)),
        ]
        args = (lam, q_arr, kc_arr, vc_arr, kn_arr, vn_arr, g)
    return pl.pallas_call(
        kern,
        grid=(batch, n_groups),
        in_specs=in_specs,
        out_specs=pl.BlockSpec((seq, w), lambda b, hg: (b, hg)),
        out_shape=jax.ShapeDtypeStruct((batch * seq, n_heads * HEAD_W), BF16),
        scratch_shapes=[pltpu.VMEM((2, hb, past, HEAD_W), F32), pltpu.VMEM((2, hb, past, HEAD_W), F32),
                        pltpu.SemaphoreType.DMA((2,))],
        compiler_params=_params(("arbitrary", "arbitrary")),
        name=mode + "_sample_attn",
    )(*args)


def _merge_kernel(ofp_ref, ofs_ref, odp_ref, ods_ref, wf_ref, wd_ref, gf_ref, gd_ref, o_ref,
                  *, n_prompt_tiles):
    i = pl.program_id(0)

    def run(of_ref, od_ref):
        a = jnp.dot(of_ref[...], wf_ref[...], preferred_element_type=F32)
        b = jnp.dot(od_ref[...], wd_ref[...], preferred_element_type=F32)
        o_ref[...] = (gf_ref[...] * a + gd_ref[...] * b).astype(o_ref.dtype)

    pl.when(i < n_prompt_tiles)(functools.partial(run, ofp_ref, odp_ref))
    pl.when(i >= n_prompt_tiles)(functools.partial(run, ofs_ref, ods_ref))


def _merge(o_fox_p, o_fox_s, o_diff_p, o_diff_s, wb_fox, wb_diff, gates, tm=1024, tn=512):
    tp, mix = o_fox_p.shape
    ts = o_fox_s.shape[0]
    d = wb_fox.shape[1]
    tm, tn = _tile(math.gcd(tp, ts), tm), _tile(d, tn)
    nj = d // tn
    npt = tp // tm
    prompt = pl.BlockSpec((tm, mix), lambda i, j: (jnp.minimum(i, npt - 1), 0))
    sample = pl.BlockSpec((tm, mix), lambda i, j: (jnp.maximum(i - npt, 0), 0))
    return pl.pallas_call(
        functools.partial(_merge_kernel, n_prompt_tiles=npt),
        grid=((tp + ts) // tm, nj),
        in_specs=[
            prompt, sample, prompt, sample,
            pl.BlockSpec((mix, tn), lambda i, j: (0, j)),
            pl.BlockSpec((mix, tn), lambda i, j: (0, j)),
            pl.BlockSpec((tm, tn), lambda i, j: (i, j)),
            pl.BlockSpec((tm, tn), lambda i, j: (i, nj + j)),
        ],
        out_specs=pl.BlockSpec((tm, tn), lambda i, j: (i, j)),
        out_shape=jax.ShapeDtypeStruct((tp + ts, d), BF16),
        compiler_params=_params(("arbitrary", "arbitrary")),
        name="branch_merge",
    )(o_fox_p, o_fox_s, o_diff_p, o_diff_s, wb_fox, wb_diff, gates, gates)


def _out_proj_kernel(a_ref, w_ref, xp_ref, xs_ref, o_ref, *, n_prompt_tiles):
    i = pl.program_id(0)
    z = jnp.dot(a_ref[...], w_ref[...], preferred_element_type=F32)

    @pl.when(i < n_prompt_tiles)
    def _():
        o_ref[...] = xp_ref[...] + z

    @pl.when(i >= n_prompt_tiles)
    def _():
        o_ref[...] = xs_ref[...] + z


def _out_proj(merged, w_out, xp, xs, tm=1024, tn=512):
    t, d = merged.shape
    tp, ts = xp.shape[0], xs.shape[0]
    tm = _tile(math.gcd(tp, ts), tm)
    tn = _tile(d, tn)
    npt = tp // tm
    return pl.pallas_call(
        functools.partial(_out_proj_kernel, n_prompt_tiles=npt),
        grid=(t // tm, d // tn),
        in_specs=[
            pl.BlockSpec((tm, d), lambda i, j: (i, 0)),
            pl.BlockSpec((d, tn), lambda i, j: (0, j)),
            pl.BlockSpec((tm, tn), lambda i, j: (jnp.minimum(i, npt - 1), j)),
            pl.BlockSpec((tm, tn), lambda i, j: (jnp.maximum(i - npt, 0), j)),
        ],
        out_specs=pl.BlockSpec((tm, tn), lambda i, j: (i, j)),
        out_shape=jax.ShapeDtypeStruct((t, d), F32),
        compiler_params=_params(("arbitrary", "arbitrary")),
        name="out_proj",
    )(merged, w_out, xp, xs)


def _router_kernel(h_ref, g_ref, w_ref, b_ref, o_ref):
    hn = _rmsnorm_rows(h_ref[...], g_ref[...])
    o_ref[...] = jnp.dot(hn, w_ref[...], preferred_element_type=F32,
                         precision=lax.Precision.HIGHEST) + b_ref[...]


def _router(h1, g, w_pad, b_pad, tm=256):
    t, d = h1.shape
    tm = _tile(t, tm)
    n = w_pad.shape[1]
    return pl.pallas_call(
        _router_kernel,
        grid=(t // tm,),
        in_specs=[
            pl.BlockSpec((tm, d), lambda i: (i, 0)),
            pl.BlockSpec((1, d), lambda i: (0, 0)),
            pl.BlockSpec((d, n), lambda i: (0, 0)),
            pl.BlockSpec((1, n), lambda i: (0, 0)),
        ],
        out_specs=pl.BlockSpec((tm, n), lambda i: (i, 0)),
        out_shape=jax.ShapeDtypeStruct((t, n), F32),
        compiler_params=_params(("arbitrary",)),
        name="router",
    )(h1, g.reshape(1, d), w_pad, b_pad)


def _gather_kernel(tok_ref, nu_ref, h_hbm, g_ref, o_ref, buf, sem):
    i = pl.program_id(0)
    n_used = nu_ref[0]

    def row_copy(tok, slot, r):
        return pltpu.make_async_copy(h_hbm.at[pl.ds(tok, 1), :], buf.at[slot, pl.ds(r, 1), :],
                                     sem.at[slot])

    def start_block(blk, slot):
        def body(r, carry):
            row_copy(tok_ref[blk * MOE_BLOCK + r], slot, r).start()
            return carry
        lax.fori_loop(0, MOE_BLOCK, body, 0, unroll=DMA_ISSUE_UNROLL)

    def wait_block(slot):
        def body(r, carry):
            row_copy(0, slot, r).wait()
            return carry
        lax.fori_loop(0, MOE_BLOCK, body, 0, unroll=DMA_ISSUE_UNROLL)

    @pl.when(i == 0)
    def _():
        start_block(0, 0)

    @pl.when(i + 1 < n_used)
    def _():
        start_block(i + 1, (i + 1) % 2)

    @pl.when(i < n_used)
    def _():
        slot = i % 2
        wait_block(slot)

        def norm_rows(c, carry):
            r0 = pl.multiple_of(c * NORM_ROWS, NORM_ROWS)
            x = buf[slot, pl.ds(r0, NORM_ROWS), :]
            o_ref[pl.ds(r0, NORM_ROWS), :] = _rmsnorm_rows(x, g_ref[...]).astype(o_ref.dtype)
            return carry
        lax.fori_loop(0, MOE_BLOCK // NORM_ROWS, norm_rows, 0, unroll=4)

    @pl.when(i >= n_used)
    def _():
        o_ref[...] = jnp.zeros(o_ref.shape, o_ref.dtype)


def _gather_norm(h1, g, row_tok, n_used, n_blocks):
    d = h1.shape[1]
    return pl.pallas_call(
        _gather_kernel,
        grid_spec=pltpu.PrefetchScalarGridSpec(
            num_scalar_prefetch=2,
            grid=(n_blocks,),
            in_specs=[
                pl.BlockSpec(memory_space=pl.ANY),
                pl.BlockSpec((1, d), lambda i, tok, nu: (0, 0)),
            ],
            out_specs=pl.BlockSpec((MOE_BLOCK, d), lambda i, tok, nu: (i, 0)),
            scratch_shapes=[pltpu.VMEM((2, MOE_BLOCK, d), F32), pltpu.SemaphoreType.DMA((2,))],
        ),
        out_shape=jax.ShapeDtypeStruct((n_blocks * MOE_BLOCK, d), BF16),
        compiler_params=_params(("arbitrary",)),
        name="moe_gather",
    )(row_tok, n_used, h1, g.reshape(1, d))


def _expert_chunk_kernel(ce_ref, cr_ref, cn_ref, meta_ref, *refs, mode, n_alloc_blocks, n_slots, nj):
    if mode == "up":
        a_hbm, wg_ref, wu_ref, bg_ref, bu_ref, o_hbm, abuf, w_s, obuf, isem, osem, zsem = refs
    else:
        a_hbm, rw_hbm, wd_ref, bd_ref, o_hbm, abuf, rwbuf, w_s, obuf, isem, osem, zsem = refs
    c = pl.program_id(0)
    j = pl.program_id(1)
    tn = obuf.shape[-1]
    mb = MOE_BLOCK
    n_chunks = meta_ref[0]
    n = c * nj + j
    last_active = n_chunks * nj - 1
    active = c < n_chunks
    nblk = cn_ref[c]
    r0 = cr_ref[c]

    def in_copies(b):
        rows_hbm = pl.ds((r0 + b) * mb, mb)
        rows_buf = pl.ds(b * mb, mb)
        cps = [pltpu.make_async_copy(a_hbm.at[rows_hbm, :], abuf.at[rows_buf, :], isem)]
        if mode == "down":
            cps.append(pltpu.make_async_copy(rw_hbm.at[rows_hbm, :], rwbuf.at[rows_buf, :], isem))
        return cps

    def out_copy(slot, row_blk, col, b):
        return pltpu.make_async_copy(
            obuf.at[slot, pl.ds(b * mb, mb), :],
            o_hbm.at[pl.ds((row_blk + b) * mb, mb), pl.ds(col * tn, tn)], osem.at[slot])

    def for_blocks(count, fn):
        for b in range(CHUNK_BLOCKS):
            pl.when(b < count)(functools.partial(fn, b))

    def start_in(b):
        for cp in in_copies(b):
            cp.start()

    def wait_in(b):
        for cp in in_copies(b):
            cp.wait()

    def wait_out(step, slot):
        cs = step // nj
        for_blocks(cn_ref[cs], lambda b: out_copy(slot, cr_ref[cs], step % nj, b).wait())

    @pl.when(active & (j == 0))
    def _():
        for_blocks(nblk, start_in)
        for_blocks(nblk, wait_in)

    @pl.when(active)
    def _():
        slot = n % 2

        @pl.when(n >= 2)
        def _():
            wait_out(n - 2, slot)

        def compute(row, m):
            if mode == "up":
                if row == 0:
                    w_s[:, :tn] = wg_ref[...].astype(BF16)
                    w_s[:, tn:] = wu_ref[...].astype(BF16)
                bias = jnp.concatenate([bg_ref[...], bu_ref[...]], axis=1)
            else:
                if row == 0:
                    w_s[...] = wd_ref[...].astype(BF16)
                bias = bd_ref[...]
            rows = pl.ds(row, m)
            z = jnp.dot(abuf[rows, :], w_s[...], preferred_element_type=F32) + bias
            if mode == "up":
                g = jnp.minimum(z[:, :tn], SWIGLU_LIMIT)
                u = jnp.clip(z[:, tn:], -SWIGLU_LIMIT, SWIGLU_LIMIT)
                obuf[slot, rows, :] = ((u + 1.0) * (g * jax.nn.sigmoid(SWIGLU_ALPHA * g))).astype(obuf.dtype)
            else:
                rw = rwbuf[rows, :]
                for lc in range(tn // LANE):
                    obuf[slot, rows, lc * LANE:(lc + 1) * LANE] = z[:, lc * LANE:(lc + 1) * LANE] * rw

        for p in range(CHUNK_BLOCKS // 2):
            pl.when(2 * p + 1 < nblk)(functools.partial(compute, 2 * p * mb, 2 * mb))
            pl.when(2 * p + 1 == nblk)(functools.partial(compute, 2 * p * mb, mb))

        for_blocks(nblk, lambda b: out_copy(slot, r0, j, b).start())

        @pl.when(n == last_active)
        def _():
            @pl.when(n >= 1)
            def _():
                wait_out(n - 1, 1 - slot)
            wait_out(n, slot)

    @pl.when((c == n_slots - 1) & (j == nj - 1))
    def _():
        tail0 = meta_ref[1]
        n_tiles = (n_alloc_blocks - tail0) * nj
        obuf[0, 0:mb, :] = jnp.zeros((mb, tn), obuf.dtype)

        def zero_copy(t):
            return pltpu.make_async_copy(
                obuf.at[0, pl.ds(0, mb), :],
                o_hbm.at[pl.ds((tail0 + t // nj) * mb, mb), pl.ds((t % nj) * tn, tn)], zsem)

        def start(t, carry):
            zero_copy(t).start()
            return carry

        def wait(t, carry):
            zero_copy(t).wait()
            return carry

        lax.fori_loop(0, n_tiles, start, 0)
        lax.fori_loop(0, n_tiles, wait, 0)


def _expert_matmul(mode, a, weights, biases, chunks, row_w=None, tn=256):
    chunk_e, chunk_r0, chunk_n, meta = chunks
    n_rows, kdim = a.shape
    n_e, _, ndim = weights[0].shape
    tn = _tile(ndim, tn)
    nj = ndim // tn
    n_slots = chunk_e.shape[0]
    cols = CHUNK_BLOCKS * MOE_BLOCK

    def wmap(c, j, ce, cr, cn, meta):
        return (ce[c], 0, jnp.where(c < meta[0], j, nj - 1))

    w_specs = [pl.BlockSpec((None, kdim, tn), wmap) for _ in weights]
    b_specs = [pl.BlockSpec((None, 1, tn), wmap) for _ in biases]
    any_spec = pl.BlockSpec(memory_space=pl.ANY)
    if mode == "up":
        in_specs = [any_spec] + w_specs + b_specs
        args = [a] + list(weights) + [b.reshape(n_e, 1, ndim) for b in biases]
        out_dtype = BF16
        scratch = [pltpu.VMEM((cols, kdim), BF16), pltpu.VMEM((kdim, 2 * tn), BF16)]
    else:
        in_specs = [any_spec, any_spec] + w_specs + b_specs
        args = [a, row_w] + list(weights) + [b.reshape(n_e, 1, ndim) for b in biases]
        out_dtype = F32
        scratch = [pltpu.VMEM((cols, kdim), BF16), pltpu.VMEM((cols, LANE), F32),
                   pltpu.VMEM((kdim, tn), BF16)]
    scratch += [pltpu.VMEM((2, cols, tn), out_dtype), pltpu.SemaphoreType.DMA(()),
                pltpu.SemaphoreType.DMA((2,)), pltpu.SemaphoreType.DMA(())]
    return pl.pallas_call(
        functools.partial(_expert_chunk_kernel, mode=mode, n_alloc_blocks=n_rows // MOE_BLOCK,
                          n_slots=n_slots, nj=nj),
        grid_spec=pltpu.PrefetchScalarGridSpec(
            num_scalar_prefetch=4,
            grid=(n_slots, nj),
            in_specs=in_specs,
            out_specs=any_spec,
            scratch_shapes=scratch,
        ),
        out_shape=jax.ShapeDtypeStruct((n_rows, ndim), out_dtype),
        compiler_params=_params(("arbitrary", "arbitrary")),
        name="moe_" + mode,
    )(chunk_e, chunk_r0, chunk_n, meta, *args)


def _combine_kernel(pos_ref, h_ref, y_hbm, g_ref, op_ref, os_ref, buf, sem, *, tt, n_prompt_tiles):
    i = pl.program_id(0)
    n = pl.num_programs(0)

    def row_copy(p, slot, k, t):
        return pltpu.make_async_copy(y_hbm.at[pl.ds(p, 1), :], buf.at[slot, k, pl.ds(t, 1), :],
                                     sem.at[slot])

    def start_tile(tile, slot):
        def body(t, carry):
            for k in range(TOP_K):
                row_copy(pos_ref[(tile * tt + t) * TOP_K + k], slot, k, t).start()
            return carry
        lax.fori_loop(0, tt, body, 0, unroll=DMA_ISSUE_UNROLL // TOP_K)

    def wait_tile(slot):
        def body(t, carry):
            for k in range(TOP_K):
                row_copy(0, slot, k, t).wait()
            return carry
        lax.fori_loop(0, tt, body, 0, unroll=DMA_ISSUE_UNROLL // TOP_K)

    @pl.when(i == 0)
    def _():
        start_tile(0, 0)

    @pl.when(i + 1 < n)
    def _():
        start_tile(i + 1, (i + 1) % 2)

    slot = i % 2
    wait_tile(slot)

    def finish(o_ref):
        def rows(c, carry):
            r0 = pl.multiple_of(c * 8, 8)
            acc = h_ref[pl.ds(r0, 8), :]
            for k in range(TOP_K):
                acc = acc + buf[slot, k, pl.ds(r0, 8), :]
            o_ref[pl.ds(r0, 8), :] = _rmsnorm_rows(acc, g_ref[...])
            return carry
        lax.fori_loop(0, tt // 8, rows, 0, unroll=4)

    @pl.when(i < n_prompt_tiles)
    def _():
        finish(op_ref)

    @pl.when(i >= n_prompt_tiles)
    def _():
        finish(os_ref)


def _combine(h1, y, pos, g, tp, tt=64):
    t, d = h1.shape
    ts = t - tp
    tt = _tile_rows(math.gcd(tp, ts), tt)
    npt = tp // tt
    return pl.pallas_call(
        functools.partial(_combine_kernel, tt=tt, n_prompt_tiles=npt),
        grid_spec=pltpu.PrefetchScalarGridSpec(
            num_scalar_prefetch=1,
            grid=(t // tt,),
            in_specs=[
                pl.BlockSpec((tt, d), lambda i, pos: (i, 0)),
                pl.BlockSpec(memory_space=pl.ANY),
                pl.BlockSpec((1, d), lambda i, pos: (0, 0)),
            ],
            out_specs=[
                pl.BlockSpec((tt, d), lambda i, pos: (jnp.minimum(i, npt - 1), 0)),
                pl.BlockSpec((tt, d), lambda i, pos: (jnp.maximum(i - npt, 0), 0)),
            ],
            scratch_shapes=[pltpu.VMEM((2, TOP_K, tt, d), F32), pltpu.SemaphoreType.DMA((2,))],
        ),
        out_shape=[jax.ShapeDtypeStruct((tp, d), F32), jax.ShapeDtypeStruct((ts, d), F32)],
        compiler_params=_params(("arbitrary",)),
        name="moe_combine",
    )(pos, h1, y, g.reshape(1, d))


def _rope_tables(pos):
    half = (HEAD_W // 2) // 8
    inv_freq = ROPE_THETA ** (-jnp.arange(half, dtype=F32) * (2.0 / (2 * half)))
    ang = pos.astype(F32)[:, None] * inv_freq[None, :]
    cos, sin = jnp.cos(ang), jnp.sin(ang)
    ones = jnp.ones((pos.shape[0], HEAD_W // 2 - 2 * half), F32)
    c64 = jnp.concatenate([cos, cos, ones], axis=1)
    s64 = jnp.concatenate([-sin, sin, 0.0 * ones], axis=1)
    return jnp.concatenate([c64, c64], axis=1), jnp.concatenate([s64, s64], axis=1)


def _routing(logits, n_experts, n_blocks):
    t = logits.shape[0]
    top_v, top_i = lax.top_k(logits, TOP_K)
    gate_w = jax.nn.softmax(top_v, axis=-1)
    flat_e = top_i.reshape(-1).astype(jnp.int32)
    flat_t = jnp.repeat(jnp.arange(t, dtype=jnp.int32), TOP_K)
    flat_w = gate_w.reshape(-1)
    order = jnp.argsort(flat_e).astype(jnp.int32)
    slot_of = jnp.argsort(order).astype(jnp.int32)
    experts = jnp.arange(n_experts, dtype=jnp.int32)
    counts = jnp.sum((flat_e[:, None] == experts[None, :]).astype(jnp.int32), axis=0)
    starts = jnp.cumsum(counts) - counts
    padded = (counts + MOE_BLOCK - 1) // MOE_BLOCK * MOE_BLOCK
    padded_end = jnp.cumsum(padded)
    padded_start = padded_end - padded
    n_rows = n_blocks * MOE_BLOCK
    rows = jnp.arange(n_rows, dtype=jnp.int32)
    row_e = jnp.minimum(jnp.sum((rows[:, None] >= padded_end[None, :]).astype(jnp.int32), axis=1),
                        n_experts - 1)
    offset = rows - padded_start[row_e]
    valid = offset < counts[row_e]
    assign = order[jnp.clip(starts[row_e] + offset, 0, t * TOP_K - 1)]
    row_tok = jnp.where(valid, flat_t[assign], 0)
    row_w = jnp.where(valid, flat_w[assign], 0.0)
    pos = padded_start[flat_e] + slot_of - starts[flat_e]
    n_used = (padded_end[-1] // MOE_BLOCK).astype(jnp.int32)
    blocks = padded // MOE_BLOCK
    per_expert = (blocks + CHUNK_BLOCKS - 1) // CHUNK_BLOCKS
    chunk_end = jnp.cumsum(per_expert)
    n_chunks = chunk_end[-1]
    n_slots = -(-n_blocks // CHUNK_BLOCKS) + n_experts
    slots = jnp.arange(n_slots, dtype=jnp.int32)
    s_eff = jnp.minimum(slots, n_chunks - 1)
    chunk_e = jnp.sum((s_eff[:, None] >= chunk_end[None, :]).astype(jnp.int32), axis=1)
    k = s_eff - (chunk_end - per_expert)[chunk_e]
    real = slots < n_chunks
    chunk_r0 = jnp.where(real, padded_start[chunk_e] // MOE_BLOCK + k * CHUNK_BLOCKS, 0)
    chunk_n = jnp.where(real, jnp.clip(blocks[chunk_e] - k * CHUNK_BLOCKS, 0, CHUNK_BLOCKS), 0)
    meta = jnp.stack([n_chunks, n_used]).astype(jnp.int32)
    chunks = (chunk_e.astype(jnp.int32), chunk_r0.astype(jnp.int32), chunk_n.astype(jnp.int32), meta)
    return row_tok, row_w, pos, chunks, n_used.reshape(1)


def kernel(x_prompt, x_sample, cache_fox_k, cache_fox_v, cache_fox_logf, cache_diff_k, cache_diff_v, norm_mix, w_in, b_forget, lambda_q1, lambda_k1, lambda_q2, lambda_k2, diff_subln, w_branch, w_out, norm_ffn, w_router, b_router, moe_w_gate, moe_b_gate, moe_w_up, moe_b_up, moe_w_down, moe_b_down, norm_final):
    batch, seq, d = x_prompt.shape
    dbatch, dseq, _ = x_sample.shape
    depth, _, past, n_heads, _ = cache_fox_k.shape
    assert depth == 1 and cache_fox_k.shape[-1] == HEAD_W and cache_diff_k.shape[-1] == HEAD_W
    assert dseq == CHUNK and past % CHUNK == 0, "sample queries must form exactly the newest chunk"
    mix = n_heads * HEAD_W
    n_experts = w_router.shape[-1]
    tp, ts = batch * seq, dbatch * dseq
    t = tp + ts
    assert (t * TOP_K) % MOE_BLOCK == 0
    xp = x_prompt.reshape(tp, d)
    xs = x_sample.reshape(ts, d)

    off_f = 3 * mix
    off_dq = off_f + n_heads
    off_dv = off_dq + 2 * mix
    off_gate = off_dv + mix
    w = w_in[0]
    w_seg = lambda col0: w[:, col0:col0 + mix].astype(BF16)
    w_fgt = jnp.pad(w[:, off_f:off_dq], ((0, 0), (0, LANE - n_heads))).astype(BF16)
    b_fgt = jnp.pad(b_forget[0], (0, LANE - n_heads)).reshape(1, LANE)
    w_gates = w[:, off_gate:].astype(BF16)

    row_tile = _tile(math.gcd(tp, ts), 1024)
    xn = _norm_pair(xp, xs, norm_mix[0], _tile(math.gcd(tp, ts), 256))

    pos_all = jnp.concatenate([jnp.tile(jnp.arange(seq), batch), jnp.tile(past + jnp.arange(dseq), dbatch)])
    cos_t, sin_t = _rope_tables(pos_all)

    seg = lambda col0, mode, extra, name: _proj(xn, w_seg(col0), mix, mode, extra=extra, tm=row_tile,
                                                name=name, split=tp)
    fq_p, fq_s = seg(0, "plain", (), "proj_fox_q")
    fk_p, fk_s = seg(mix, "plain", (), "proj_fox_k")
    fv_p, fv_s = seg(2 * mix, "plain", (), "proj_fox_v")
    dq_p, dq_s = seg(off_dq, "rope", (cos_t, sin_t), "proj_diff_q")
    dk_p, dk_s = seg(off_dq + mix, "rope", (cos_t, sin_t), "proj_diff_k")
    dv_p, dv_s = seg(off_dv, "plain", (), "proj_diff_v")
    logf = _proj(xn, w_fgt, LANE, "logsig", extra=(b_fgt,), tm=row_tile, name="proj_forget")[:, :n_heads]
    gates = _proj(xn, w_gates, 2 * d, "sigmoid", tm=row_tile, name="proj_gates")

    logf_p = logf[:tp].reshape(batch, seq, n_heads)
    logf_s = logf[tp:].reshape(dbatch, dseq, n_heads)
    ck_p_flat = _cumsum_seq(logf_p)
    ck_s = _cumsum_seq(jnp.concatenate([cache_fox_logf[0].astype(F32), logf_s], axis=1))
    cum_p = ck_p_flat.transpose(0, 2, 1)
    tq = _tile(seq, 512)
    ck_p = ck_p_flat.reshape(batch, n_heads, seq // tq, tq).transpose(0, 2, 1, 3)
    cq_s = ck_s[:, :, past:].transpose(0, 2, 1).reshape(ts, n_heads)

    f32 = F32
    lam_init = 0.8 - 0.6 * math.exp(-0.3 * 0)
    lam = (jnp.exp(jnp.sum(lambda_q1[0].astype(f32) * lambda_k1[0].astype(f32)))
           - jnp.exp(jnp.sum(lambda_q2[0].astype(f32) * lambda_k2[0].astype(f32))) + lam_init).reshape(1)
    subln = diff_subln[0].reshape(1, HEAD_W)
    nh = n_heads
    fox_scale = HEAD_W ** -0.5
    diff_scale = (HEAD_W // 2) ** -0.5

    o_fox_p = _prompt_attention("fox", fq_p, 0, fk_p, 0, fv_p, 0, batch, seq, nh,
                                (cum_p.reshape(tp, nh), ck_p), fox_scale)
    o_diff_p = _prompt_attention("diff", dq_p, 0, dk_p, 0, dv_p, 0, batch, seq, nh,
                                 (lam, subln), diff_scale, post_scale=1.0 - lam_init)
    o_fox_s = _sample_attention("fox", fq_s, 0, cache_fox_k[0], cache_fox_v[0], fk_s, 0, fv_s, 0,
                                0, dbatch, dseq, past, nh,
                                (cq_s, ck_s[:, :, :past], ck_s[:, :, past:]), fox_scale)
    o_diff_s = _sample_attention("diff", dq_s, 0, cache_diff_k[0], cache_diff_v[0], dk_s, 0, dv_s, 0,
                                 0, dbatch, dseq, past, nh, (lam, subln), diff_scale,
                                 post_scale=1.0 - lam_init)

    merged = _merge(o_fox_p, o_fox_s, o_diff_p, o_diff_s, w_branch[0, 0].astype(BF16),
                    w_branch[0, 1].astype(BF16), gates, tm=row_tile)
    h1 = _out_proj(merged, w_out[0].astype(BF16), xp, xs, tm=row_tile)

    w_r = jnp.pad(w_router[0], ((0, 0), (0, LANE - n_experts)))
    b_r = jnp.pad(b_router[0].astype(F32), (0, LANE - n_experts)).reshape(1, LANE)
    logits = _router(h1, norm_ffn[0], w_r, b_r)[:, :n_experts]
    n_blocks = (t * TOP_K) // MOE_BLOCK + n_experts
    row_tok, row_w, pos, chunks, n_used = _routing(logits, n_experts, n_blocks)
    xs_sorted = _gather_norm(h1, norm_ffn[0], row_tok, n_used, n_blocks)
    hmid = _expert_matmul("up", xs_sorted, [moe_w_gate[0], moe_w_up[0]], [moe_b_gate[0], moe_b_up[0]],
                          chunks, tn=256)
    row_w_b = jnp.broadcast_to(row_w[:, None], (row_w.shape[0], LANE))
    y_rows = _expert_matmul("down", hmid, [moe_w_down[0]], [moe_b_down[0]], chunks, row_w=row_w_b,
                            tn=512)
    y_p, y_s = _combine(h1, y_rows, pos, norm_final, tp)

    def state(arr, b, s):
        return arr.reshape(1, b, s, n_heads, HEAD_W)

    outs_p = (state(fk_p, batch, seq), state(fv_p, batch, seq), logf_p[None],
              state(dk_p, batch, seq), state(dv_p, batch, seq))
    outs_s = (state(fk_s, dbatch, dseq), state(fv_s, dbatch, dseq), logf_s[None],
              state(dk_s, dbatch, dseq), state(dv_s, dbatch, dseq))
    return (y_p.reshape(batch, seq, d), y_s.reshape(dbatch, dseq, d)) + outs_p + outs_s
```

```python
import functools
import math

import jax
import jax.numpy as jnp
from jax import lax
from jax.experimental import pallas as pl
from jax.experimental.pallas import tpu as pltpu

F32 = jnp.float32
BF16 = jnp.bfloat16

CHUNK = 64
ROPE_THETA = 500000.0
TOP_K = 4
SWIGLU_LIMIT = 7.0
SWIGLU_ALPHA = 1.702
MOE_BLOCK = 256
CHUNK_BLOCKS = 8
DMA_ISSUE_UNROLL = 8
NORM_ROWS = 16
NORM_EPS = 1e-5
LANE = 128
HEAD_W = 128
VMEM_LIMIT = 56 * 1024 * 1024
NEG_INIT = -1e30
ATTN_STRIP = 256


def _tile(n, pref):
    if n <= LANE:
        return n
    t = min(pref, n) // LANE * LANE
    while n % t:
        t -= LANE
    return t


def _tile_rows(n, pref):
    t = min(pref, n) // 8 * 8
    while n % t:
        t -= 8
    return t


def _params(sem):
    return pltpu.CompilerParams(dimension_semantics=sem, vmem_limit_bytes=VMEM_LIMIT)


def _rmsnorm_rows(x, g):
    ms = jnp.mean(x * x, axis=-1, keepdims=True)
    return x * lax.rsqrt(ms + NORM_EPS) * g


def _norm_pair_kernel(xp_ref, xs_ref, g_ref, o_ref, *, n_prompt_tiles):
    i = pl.program_id(0)

    @pl.when(i < n_prompt_tiles)
    def _():
        o_ref[...] = _rmsnorm_rows(xp_ref[...], g_ref[...]).astype(o_ref.dtype)

    @pl.when(i >= n_prompt_tiles)
    def _():
        o_ref[...] = _rmsnorm_rows(xs_ref[...], g_ref[...]).astype(o_ref.dtype)


def _norm_pair(xp, xs, g, tm):
    tp, d = xp.shape
    ts = xs.shape[0]
    npt, nst = tp // tm, ts // tm
    return pl.pallas_call(
        functools.partial(_norm_pair_kernel, n_prompt_tiles=npt),
        grid=(npt + nst,),
        in_specs=[
            pl.BlockSpec((tm, d), lambda i: (jnp.minimum(i, npt - 1), 0)),
            pl.BlockSpec((tm, d), lambda i: (jnp.maximum(i - npt, 0), 0)),
            pl.BlockSpec((1, d), lambda i: (0, 0)),
        ],
        out_specs=pl.BlockSpec((tm, d), lambda i: (i, 0)),
        out_shape=jax.ShapeDtypeStruct((tp + ts, d), BF16),
        compiler_params=_params(("arbitrary",)),
        name="norm_mix",
    )(xp, xs, g.reshape(1, d))


def _proj_kernel(a_ref, w_ref, *rest, mode, n_prompt_tiles):
    n_out = 1 if n_prompt_tiles is None else 2
    extra, outs = rest[:len(rest) - n_out], rest[len(rest) - n_out:]
    z = jnp.dot(a_ref[...], w_ref[...], preferred_element_type=F32)

    def store(o_ref):
        if mode == "plain":
            o_ref[...] = z
        elif mode == "sigmoid":
            o_ref[...] = jax.nn.sigmoid(z)
        elif mode == "logsig":
            (b_ref,) = extra
            o_ref[...] = jax.nn.log_sigmoid(z + b_ref[...])
        else:
            c_ref, s_ref = extra
            tm, tn = z.shape
            lane = lax.broadcasted_iota(jnp.int32, (tm, LANE), 1)
            first_half = (lane % 64) < 8
            cos, sin = c_ref[...], s_ref[...]
            for c in range(tn // LANE):
                zc = z[:, c * LANE:(c + 1) * LANE]
                rot = jnp.where(first_half, pltpu.roll(zc, LANE - 8, 1), pltpu.roll(zc, 8, 1))
                o_ref[:, c * LANE:(c + 1) * LANE] = zc * cos + rot * sin

    if n_prompt_tiles is None:
        store(outs[0])
    else:
        i = pl.program_id(0)
        pl.when(i < n_prompt_tiles)(functools.partial(store, outs[0]))
        pl.when(i >= n_prompt_tiles)(functools.partial(store, outs[1]))


def _proj(a, w, n_cols, mode, extra=(), tm=1024, tn=512, name="proj", split=None):
    t, d = a.shape
    tm = _tile(t, tm)
    tn = _tile(n_cols, tn)
    nj = n_cols // tn
    in_specs = [
        pl.BlockSpec((tm, d), lambda i, j: (i, 0)),
        pl.BlockSpec((d, tn), lambda i, j: (0, j)),
    ]
    if mode == "logsig":
        in_specs.append(pl.BlockSpec((1, tn), lambda i, j: (0, j)))
    elif mode == "rope":
        in_specs += [pl.BlockSpec((tm, LANE), lambda i, j: (i, 0))] * 2
    if split is None:
        npt = None
        out_specs = pl.BlockSpec((tm, tn), lambda i, j: (i, j))
        out_shape = jax.ShapeDtypeStruct((t, n_cols), F32)
    else:
        assert split % tm == 0
        npt = split // tm
        out_specs = [
            pl.BlockSpec((tm, tn), lambda i, j: (jnp.minimum(i, npt - 1), jnp.where(i < npt, j, nj - 1))),
            pl.BlockSpec((tm, tn), lambda i, j: (jnp.maximum(i - npt, 0), jnp.where(i >= npt, j, 0))),
        ]
        out_shape = [jax.ShapeDtypeStruct((split, n_cols), F32),
                     jax.ShapeDtypeStruct((t - split, n_cols), F32)]
    return pl.pallas_call(
        functools.partial(_proj_kernel, mode=mode, n_prompt_tiles=npt),
        grid=(t // tm, nj),
        in_specs=in_specs,
        out_specs=out_specs,
        out_shape=out_shape,
        compiler_params=_params(("arbitrary", "arbitrary")),
        name=name,
    )(a, w, *extra)


def _cumsum_kernel(x_ref, o_ref):
    rows, w = x_ref.shape
    lane = lax.broadcasted_iota(jnp.int32, (rows, LANE), 1)
    total = jnp.zeros((rows, 1), F32)
    for c in range(w // LANE):
        v = x_ref[:, c * LANE:(c + 1) * LANE]
        shift = 1
        while shift < LANE:
            v = v + jnp.where(lane >= shift, pltpu.roll(v, shift, 1), 0.0)
            shift *= 2
        v = v + total
        o_ref[:, c * LANE:(c + 1) * LANE] = v
        total = v[:, LANE - 1:LANE]


def _cumsum_seq(x):
    b, s, h = x.shape
    w = -(-s // LANE) * LANE
    xt = jnp.pad(x.transpose(0, 2, 1), ((0, 0), (0, 0), (0, w - s)))
    out = pl.pallas_call(
        _cumsum_kernel,
        grid=(b,),
        in_specs=[pl.BlockSpec((None, h, w), lambda i: (i, 0, 0))],
        out_specs=pl.BlockSpec((None, h, w), lambda i: (i, 0, 0)),
        out_shape=jax.ShapeDtypeStruct((b, h, w), F32),
        compiler_params=_params(("arbitrary",)),
        name="cumsum_logf",
    )(xt)
    return out[:, :, :s]


def _qk(q, k):
    return lax.dot_general(q, k, (((1,), (1,)), ((), ())), preferred_element_type=F32)


def _pick_lane(x, idx):
    lane = lax.broadcasted_iota(jnp.int32, x.shape, 1)
    return jnp.sum(jnp.where(lane == idx, x, 0.0), axis=1, keepdims=True)


def _diff_finish(o1, o2, lam, g, post_scale):
    o = o1 - lam * o2
    return _rmsnorm_rows(o, g) * post_scale


def _prompt_attn_kernel(*refs, mode, tq, rs, hb, scale, post_scale):
    if mode == "fox":
        q_ref, k_ref, v_ref, cq_ref, ck_ref, o_ref, q_s, kT_s, v_s, m_s, acc_s = refs
    else:
        lam_ref, q_ref, k_ref, v_ref, g_ref, o_ref, q_s, kT_s, v_s, m_s, acc_s = refs
    hg = pl.program_id(1)
    qi = pl.program_id(2)
    nk = kT_s.shape[1]
    seq = v_s.shape[1]
    maps_per_head = 1 if mode == "fox" else 2

    @pl.when(qi == 0)
    def _():
        for hh in range(hb):
            sl = slice(hh * HEAD_W, (hh + 1) * HEAD_W)
            v_s[hh, :, :HEAD_W] = v_ref[:, sl].astype(BF16)
            v_s[hh, :, HEAD_W:] = jnp.ones((seq, HEAD_W), BF16)
            for kj in range(nk):
                kT_s[hh, kj] = k_ref[kj * tq:(kj + 1) * tq, sl].T.astype(BF16)

    m_s[...] = jnp.full(m_s.shape, NEG_INIT, F32)
    acc_s[...] = jnp.zeros(acc_s.shape, F32)

    cqs = []
    for hh in range(hb):
        q = q_ref[:, hh * HEAD_W:(hh + 1) * HEAD_W] * scale
        if mode == "fox":
            q_s[hh] = q.astype(BF16)
            cqs.append(_pick_lane(cq_ref[...], hg * hb + hh))
        else:
            lane = lax.broadcasted_iota(jnp.int32, q.shape, 1)
            q_s[2 * hh] = jnp.where(lane < HEAD_W // 2, q, 0.0).astype(BF16)
            q_s[2 * hh + 1] = jnp.where(lane >= HEAD_W // 2, q, 0.0).astype(BF16)

    def tile(kj, diagonal):
        ks = pl.multiple_of(kj * tq, tq)
        for r0 in range(0, tq, rs):
            ncol = r0 + rs if diagonal else tq
            if diagonal:
                r = r0 + lax.broadcasted_iota(jnp.int32, (rs, ncol), 0)
                c = lax.broadcasted_iota(jnp.int32, (rs, ncol), 1)
                visible = (c <= r) if mode == "fox" else ((c // CHUNK) <= (r // CHUNK))
            for mi in range(hb * maps_per_head):
                hh = mi // maps_per_head
                s = jnp.dot(q_s[mi, r0:r0 + rs, :], kT_s[hh, kj, :, :ncol], preferred_element_type=F32)
                if mode == "fox":
                    ck = ck_ref[kj, pl.ds(hg * hb + hh, 1), :]
                    s = s + (cqs[hh][r0:r0 + rs] - ck[:, :ncol])
                if diagonal:
                    s = jnp.where(visible, s, -jnp.inf)
                m_prev = m_s[mi, r0:r0 + rs, :]
                m_new = jnp.maximum(m_prev, jnp.max(s, axis=-1, keepdims=True))
                alpha = jnp.exp(m_prev - m_new)
                p = jnp.exp(s - m_new[:, :1])
                pv = jnp.dot(p.astype(BF16), v_s[hh, pl.ds(ks, ncol), :], preferred_element_type=F32)
                acc_s[mi, r0:r0 + rs, :] = (jnp.concatenate([alpha, alpha], axis=1) * acc_s[mi, r0:r0 + rs, :]
                                            + pv)
                m_s[mi, r0:r0 + rs, :] = m_new

    def body(kj, carry):
        tile(kj, False)
        return carry

    lax.fori_loop(0, qi, body, 0)
    tile(qi, True)

    for hh in range(hb):
        sl = slice(hh * HEAD_W, (hh + 1) * HEAD_W)
        if mode == "fox":
            a = acc_s[hh]
            o = a[:, :HEAD_W] / a[:, HEAD_W:]
        else:
            a1, a2 = acc_s[2 * hh], acc_s[2 * hh + 1]
            o = _diff_finish(a1[:, :HEAD_W] / a1[:, HEAD_W:], a2[:, :HEAD_W] / a2[:, HEAD_W:],
                             lam_ref[0], g_ref[...], post_scale)
        o_ref[:, sl] = o.astype(o_ref.dtype)


def _prompt_attention(mode, q_arr, q_col0, k_arr, k_col0, v_arr, v_col0, batch, seq, n_heads,
                      extra, scale, post_scale=None):
    tq = _tile(seq, 512)
    nq = seq // tq
    hb = 4 if n_heads % 4 == 0 else (2 if n_heads % 2 == 0 else 1)
    w = hb * HEAD_W
    n_maps = hb if mode == "fox" else 2 * hb
    row_q = lambda b, hg, qi: (b * nq + qi, q_col0 // hb + hg)
    kv_spec = lambda col0: pl.BlockSpec((seq, w), lambda b, hg, qi: (b, col0 // hb + hg))
    if mode == "fox":
        cq, ck = extra
        in_specs = [
            pl.BlockSpec((tq, w), row_q), kv_spec(k_col0), kv_spec(v_col0),
            pl.BlockSpec((tq, n_heads), lambda b, hg, qi: (b * nq + qi, 0)),
            pl.BlockSpec((None, nq, n_heads, tq), lambda b, hg, qi: (b, 0, 0, 0)),
        ]
        args = (q_arr, k_arr, v_arr, cq, ck)
    else:
        lam, g = extra
        in_specs = [
            pl.BlockSpec(memory_space=pltpu.SMEM),
            pl.BlockSpec((tq, w), row_q), kv_spec(k_col0), kv_spec(v_col0),
            pl.BlockSpec((1, HEAD_W), lambda b, hg, qi: (0, 0)),
        ]
        args = (lam, q_arr, k_arr, v_arr, g)
    return pl.pallas_call(
        functools.partial(_prompt_attn_kernel, mode=mode, tq=tq, rs=min(ATTN_STRIP, tq), hb=hb,
                          scale=scale, post_scale=post_scale),
        grid=(batch, n_heads // hb, nq),
        in_specs=in_specs,
        out_specs=pl.BlockSpec((tq, w), lambda b, hg, qi: (b * nq + qi, hg)),
        out_shape=jax.ShapeDtypeStruct((batch * seq, n_heads * HEAD_W), BF16),
        scratch_shapes=[
            pltpu.VMEM((n_maps, tq, HEAD_W), BF16),
            pltpu.VMEM((hb, nq, HEAD_W, tq), BF16),
            pltpu.VMEM((hb, seq, 2 * HEAD_W), BF16),
            pltpu.VMEM((n_maps, tq, LANE), F32),
            pltpu.VMEM((n_maps, tq, 2 * HEAD_W), F32),
        ],
        compiler_params=_params(("arbitrary", "arbitrary", "arbitrary")),
        name=mode + "_prompt_attn",
    )(*args)


def _cache_fetch(kc_hbm, vc_hbm, kbuf, vbuf, sem, heads_per_step, n_groups, n_steps):
    n = pl.program_id(0) * n_groups + pl.program_id(1)

    def copies(step, slot):
        b = step // n_groups
        h0 = (step % n_groups) * heads_per_step
        cps = []
        for hh in range(heads_per_step):
            cps.append(pltpu.make_async_copy(kc_hbm.at[b, :, h0 + hh, :], kbuf.at[slot, hh], sem.at[slot]))
            cps.append(pltpu.make_async_copy(vc_hbm.at[b, :, h0 + hh, :], vbuf.at[slot, hh], sem.at[slot]))
        return cps

    @pl.when(n == 0)
    def _():
        for cp in copies(0, 0):
            cp.start()

    @pl.when(n + 1 < n_steps)
    def _():
        for cp in copies(n + 1, (n + 1) % 2):
            cp.start()

    slot = n % 2
    for cp in copies(n, slot):
        cp.wait()
    return slot


def _fox_sample_kernel(q_ref, kc_hbm, vc_hbm, kn_ref, vn_ref, cq_ref, ckc_ref, ckn_ref, o_ref,
                       kbuf, vbuf, sem, *, heads_per_step, n_groups, n_steps, scale):
    hg = pl.program_id(1)
    slot = _cache_fetch(kc_hbm, vc_hbm, kbuf, vbuf, sem, heads_per_step, n_groups, n_steps)
    for hh in range(heads_per_step):
        sl = slice(hh * HEAD_W, (hh + 1) * HEAD_W)
        h = hg * heads_per_step + hh
        q = (q_ref[:, sl] * scale).astype(BF16)
        cq = _pick_lane(cq_ref[...], h)
        sc = _qk(q, kbuf[slot, hh].astype(BF16)) + cq - ckc_ref[pl.ds(h, 1), :]
        sn = _qk(q, kn_ref[:, sl].astype(BF16)) + cq - ckn_ref[pl.ds(h, 1), :]
        r = lax.broadcasted_iota(jnp.int32, sn.shape, 0)
        c = lax.broadcasted_iota(jnp.int32, sn.shape, 1)
        sn = jnp.where(c <= r, sn, -jnp.inf)
        m = jnp.maximum(jnp.max(sc, axis=-1, keepdims=True), jnp.max(sn, axis=-1, keepdims=True))
        pc = jnp.exp(sc - m)
        pn = jnp.exp(sn - m)
        l = jnp.sum(pc, axis=-1, keepdims=True) + jnp.sum(pn, axis=-1, keepdims=True)
        o = (jnp.dot(pc.astype(BF16), vbuf[slot, hh].astype(BF16), preferred_element_type=F32)
             + jnp.dot(pn.astype(BF16), vn_ref[:, sl].astype(BF16), preferred_element_type=F32))
        o_ref[:, sl] = (o / l).astype(o_ref.dtype)


def _diff_sample_kernel(lam_ref, q_ref, kc_hbm, vc_hbm, kn_ref, vn_ref, g_ref, o_ref,
                        kbuf, vbuf, sem, *, heads_per_step, n_groups, n_steps, scale, post_scale):
    slot = _cache_fetch(kc_hbm, vc_hbm, kbuf, vbuf, sem, heads_per_step, n_groups, n_steps)
    for hh in range(heads_per_step):
        sl = slice(hh * HEAD_W, (hh + 1) * HEAD_W)
        q = q_ref[:, sl] * scale
        lane = lax.broadcasted_iota(jnp.int32, q.shape, 1)
        kc = kbuf[slot, hh].astype(BF16)
        kn = kn_ref[:, sl].astype(BF16)
        vc = vbuf[slot, hh].astype(BF16)
        vn = vn_ref[:, sl].astype(BF16)
        outs = []
        for keep in (lane < HEAD_W // 2, lane >= HEAD_W // 2):
            qh = jnp.where(keep, q, 0.0).astype(BF16)
            sc = _qk(qh, kc)
            sn = _qk(qh, kn)
            m = jnp.maximum(jnp.max(sc, axis=-1, keepdims=True), jnp.max(sn, axis=-1, keepdims=True))
            pc = jnp.exp(sc - m)
            pn = jnp.exp(sn - m)
            l = jnp.sum(pc, axis=-1, keepdims=True) + jnp.sum(pn, axis=-1, keepdims=True)
            o = (jnp.dot(pc.astype(BF16), vc, preferred_element_type=F32)
                 + jnp.dot(pn.astype(BF16), vn, preferred_element_type=F32))
            outs.append(o / l)
        o = _diff_finish(outs[0], outs[1], lam_ref[0], g_ref[...], post_scale)
        o_ref[:, sl] = o.astype(o_ref.dtype)


def _sample_attention(mode, q_arr, q_col0, kc_arr, vc_arr, kn_arr, kn_col0, vn_arr, vn_col0,
                      row0, batch, seq, past, n_heads, extra, scale, post_scale=None):
    hb = 4 if n_heads % 4 == 0 else 1
    w = hb * HEAD_W
    rb0 = row0 // seq
    n_groups = n_heads // hb
    ring = dict(heads_per_step=hb, n_groups=n_groups, n_steps=batch * n_groups)
    new = lambda col0: pl.BlockSpec((seq, w), lambda b, hg: (rb0 + b, col0 // hb + hg))
    cache = pl.BlockSpec(memory_space=pl.ANY)
    if mode == "fox":
        cq, ckc, ckn = extra
        kern = functools.partial(_fox_sample_kernel, scale=scale, **ring)
        in_specs = [
            new(q_col0), cache, cache, new(kn_col0), new(vn_col0),
            pl.BlockSpec((seq, n_heads), lambda b, hg: (b, 0)),
            pl.BlockSpec((None, n_heads, past), lambda b, hg: (b, 0, 0)),
            pl.BlockSpec((None, n_heads, seq), lambda b, hg: (b, 0, 0)),
        ]
        args = (q_arr, kc_arr, vc_arr, kn_arr, vn_arr, cq, ckc, ckn)
    else:
        lam, g = extra
        kern = functools.partial(_diff_sample_kernel, scale=scale, post_scale=post_scale, **ring)
        in_specs = [
            pl.BlockSpec(memory_space=pltpu.SMEM),
            new(q_col0), cache, cache, new(kn_col0), new(vn_col0),
            pl.BlockSpec((1, HEAD_W), lambda b, hg: (0, 0)),
        ]
        args = (lam, q_arr, kc_arr, vc_arr, kn_arr, vn_arr, g)
    return pl.pallas_call(
        kern,
        grid=(batch, n_groups),
        in_specs=in_specs,
        out_specs=pl.BlockSpec((seq, w), lambda b, hg: (b, hg)),
        out_shape=jax.ShapeDtypeStruct((batch * seq, n_heads * HEAD_W), BF16),
        scratch_shapes=[pltpu.VMEM((2, hb, past, HEAD_W), F32), pltpu.VMEM((2, hb, past, HEAD_W), F32),
                        pltpu.SemaphoreType.DMA((2,))],
        compiler_params=_params(("arbitrary", "arbitrary")),
        name=mode + "_sample_attn",
    )(*args)


def _merge_kernel(ofp_ref, ofs_ref, odp_ref, ods_ref, wf_ref, wd_ref, gf_ref, gd_ref, o_ref,
                  *, n_prompt_tiles):
    i = pl.program_id(0)

    def run(of_ref, od_ref):
        a = jnp.dot(of_ref[...], wf_ref[...], preferred_element_type=F32)
        b = jnp.dot(od_ref[...], wd_ref[...], preferred_element_type=F32)
        o_ref[...] = (gf_ref[...] * a + gd_ref[...] * b).astype(o_ref.dtype)

    pl.when(i < n_prompt_tiles)(functools.partial(run, ofp_ref, odp_ref))
    pl.when(i >= n_prompt_tiles)(functools.partial(run, ofs_ref, ods_ref))


def _merge(o_fox_p, o_fox_s, o_diff_p, o_diff_s, wb_fox, wb_diff, gates, tm=1024, tn=512):
    tp, mix = o_fox_p.shape
    ts = o_fox_s.shape[0]
    d = wb_fox.shape[1]
    tm, tn = _tile(math.gcd(tp, ts), tm), _tile(d, tn)
    nj = d // tn
    npt = tp // tm
    prompt = pl.BlockSpec((tm, mix), lambda i, j: (jnp.minimum(i, npt - 1), 0))
    sample = pl.BlockSpec((tm, mix), lambda i, j: (jnp.maximum(i - npt, 0), 0))
    return pl.pallas_call(
        functools.partial(_merge_kernel, n_prompt_tiles=npt),
        grid=((tp + ts) // tm, nj),
        in_specs=[
            prompt, sample, prompt, sample,
            pl.BlockSpec((mix, tn), lambda i, j: (0, j)),
            pl.BlockSpec((mix, tn), lambda i, j: (0, j)),
            pl.BlockSpec((tm, tn), lambda i, j: (i, j)),
            pl.BlockSpec((tm, tn), lambda i, j: (i, nj + j)),
        ],
        out_specs=pl.BlockSpec((tm, tn), lambda i, j: (i, j)),
        out_shape=jax.ShapeDtypeStruct((tp + ts, d), BF16),
        compiler_params=_params(("arbitrary", "arbitrary")),
        name="branch_merge",
    )(o_fox_p, o_fox_s, o_diff_p, o_diff_s, wb_fox, wb_diff, gates, gates)


def _out_proj_kernel(a_ref, w_ref, xp_ref, xs_ref, o_ref, *, n_prompt_tiles):
    i = pl.program_id(0)
    z = jnp.dot(a_ref[...], w_ref[...], preferred_element_type=F32)

    @pl.when(i < n_prompt_tiles)
    def _():
        o_ref[...] = xp_ref[...] + z

    @pl.when(i >= n_prompt_tiles)
    def _():
        o_ref[...] = xs_ref[...] + z


def _out_proj(merged, w_out, xp, xs, tm=1024, tn=512):
    t, d = merged.shape
    tp, ts = xp.shape[0], xs.shape[0]
    tm = _tile(math.gcd(tp, ts), tm)
    tn = _tile(d, tn)
    npt = tp // tm
    return pl.pallas_call(
        functools.partial(_out_proj_kernel, n_prompt_tiles=npt),
        grid=(t // tm, d // tn),
        in_specs=[
            pl.BlockSpec((tm, d), lambda i, j: (i, 0)),
            pl.BlockSpec((d, tn), lambda i, j: (0, j)),
            pl.BlockSpec((tm, tn), lambda i, j: (jnp.minimum(i, npt - 1), j)),
            pl.BlockSpec((tm, tn), lambda i, j: (jnp.maximum(i - npt, 0), j)),
        ],
        out_specs=pl.BlockSpec((tm, tn), lambda i, j: (i, j)),
        out_shape=jax.ShapeDtypeStruct((t, d), F32),
        compiler_params=_params(("arbitrary", "arbitrary")),
        name="out_proj",
    )(merged, w_out, xp, xs)


def _router_kernel(h_ref, g_ref, w_ref, b_ref, o_ref, p_ref):
    hn = _rmsnorm_rows(h_ref[...], g_ref[...])
    o_ref[...] = jnp.dot(hn, w_ref[...], preferred_element_type=F32,
                         precision=lax.Precision.HIGHEST) + b_ref[...]
    half = hn.shape[1] // 2
    lo = lax.bitcast_convert_type(hn[:, :half].astype(BF16).astype(F32), jnp.uint32)
    hi = lax.bitcast_convert_type(hn[:, half:].astype(BF16).astype(F32), jnp.uint32)
    p_ref[...] = (lo >> 16) | (hi & jnp.uint32(0xFFFF0000))


def _router(h1, g, w_pad, b_pad, tm=256):
    t, d = h1.shape
    tm = _tile(t, tm)
    n = w_pad.shape[1]
    return pl.pallas_call(
        _router_kernel,
        grid=(t // tm,),
        in_specs=[
            pl.BlockSpec((tm, d), lambda i: (i, 0)),
            pl.BlockSpec((1, d), lambda i: (0, 0)),
            pl.BlockSpec((d, n), lambda i: (0, 0)),
            pl.BlockSpec((1, n), lambda i: (0, 0)),
        ],
        out_specs=[pl.BlockSpec((tm, n), lambda i: (i, 0)), pl.BlockSpec((tm, d // 2), lambda i: (i, 0))],
        out_shape=[jax.ShapeDtypeStruct((t, n), F32), jax.ShapeDtypeStruct((t, d // 2), jnp.uint32)],
        compiler_params=_params(("arbitrary",)),
        name="router",
    )(h1, g.reshape(1, d), w_pad, b_pad)


def _gather_kernel(tok_ref, nu_ref, h_hbm, o_ref, buf, sem):
    i = pl.program_id(0)
    n_used = nu_ref[0]

    def row_copy(tok, slot, r):
        return pltpu.make_async_copy(h_hbm.at[pl.ds(tok, 1), :], buf.at[slot, pl.ds(r, 1), :],
                                     sem.at[slot])

    def start_block(blk, slot):
        def body(r, carry):
            row_copy(tok_ref[blk * MOE_BLOCK + r], slot, r).start()
            return carry
        lax.fori_loop(0, MOE_BLOCK, body, 0, unroll=DMA_ISSUE_UNROLL)

    def wait_block(slot):
        def body(r, carry):
            row_copy(0, slot, r).wait()
            return carry
        lax.fori_loop(0, MOE_BLOCK, body, 0, unroll=DMA_ISSUE_UNROLL)

    @pl.when(i == 0)
    def _():
        start_block(0, 0)

    @pl.when(i + 1 < n_used)
    def _():
        start_block(i + 1, (i + 1) % 2)

    @pl.when(i < n_used)
    def _():
        slot = i % 2
        wait_block(slot)

        half = buf.shape[-1]

        def unpack_rows(c, carry):
            r0 = pl.multiple_of(c * NORM_ROWS, NORM_ROWS)
            w = buf[slot, pl.ds(r0, NORM_ROWS), :]
            lo = lax.bitcast_convert_type(w << 16, F32)
            hi = lax.bitcast_convert_type(w & jnp.uint32(0xFFFF0000), F32)
            o_ref[pl.ds(r0, NORM_ROWS), :half] = lo.astype(o_ref.dtype)
            o_ref[pl.ds(r0, NORM_ROWS), half:] = hi.astype(o_ref.dtype)
            return carry
        lax.fori_loop(0, MOE_BLOCK // NORM_ROWS, unpack_rows, 0, unroll=4)

    @pl.when(i >= n_used)
    def _():
        o_ref[...] = jnp.zeros(o_ref.shape, o_ref.dtype)


def _gather_rows(hn_packed, row_tok, n_used, n_blocks):
    half = hn_packed.shape[1]
    return pl.pallas_call(
        _gather_kernel,
        grid_spec=pltpu.PrefetchScalarGridSpec(
            num_scalar_prefetch=2,
            grid=(n_blocks,),
            in_specs=[pl.BlockSpec(memory_space=pl.ANY)],
            out_specs=pl.BlockSpec((MOE_BLOCK, 2 * half), lambda i, tok, nu: (i, 0)),
            scratch_shapes=[pltpu.VMEM((2, MOE_BLOCK, half), jnp.uint32), pltpu.SemaphoreType.DMA((2,))],
        ),
        out_shape=jax.ShapeDtypeStruct((n_blocks * MOE_BLOCK, 2 * half), BF16),
        compiler_params=_params(("arbitrary",)),
        name="moe_gather",
    )(row_tok, n_used, hn_packed)


def _expert_chunk_kernel(ce_ref, cr_ref, cn_ref, meta_ref, *refs, mode, n_alloc_blocks, n_slots, nj):
    if mode == "up":
        a_hbm, wg_ref, wu_ref, bg_ref, bu_ref, o_hbm, abuf, w_s, obuf, isem, osem, zsem = refs
    else:
        a_hbm, rw_hbm, wd_ref, bd_ref, o_hbm, abuf, rwbuf, w_s, obuf, isem, osem, zsem = refs
    c = pl.program_id(0)
    j = pl.program_id(1)
    tn = obuf.shape[-1]
    mb = MOE_BLOCK
    n_chunks = meta_ref[0]
    n = c * nj + j
    last_active = n_chunks * nj - 1
    active = c < n_chunks
    nblk = cn_ref[c]
    r0 = cr_ref[c]

    def in_copies(b):
        rows_hbm = pl.ds((r0 + b) * mb, mb)
        rows_buf = pl.ds(b * mb, mb)
        cps = [pltpu.make_async_copy(a_hbm.at[rows_hbm, :], abuf.at[rows_buf, :], isem)]
        if mode == "down":
            cps.append(pltpu.make_async_copy(rw_hbm.at[rows_hbm, :], rwbuf.at[rows_buf, :], isem))
        return cps

    def out_copy(slot, row_blk, col, b):
        return pltpu.make_async_copy(
            obuf.at[slot, pl.ds(b * mb, mb), :],
            o_hbm.at[pl.ds((row_blk + b) * mb, mb), pl.ds(col * tn, tn)], osem.at[slot])

    def for_blocks(count, fn):
        for b in range(CHUNK_BLOCKS):
            pl.when(b < count)(functools.partial(fn, b))

    def start_in(b):
        for cp in in_copies(b):
            cp.start()

    def wait_in(b):
        for cp in in_copies(b):
            cp.wait()

    def wait_out(step, slot):
        cs = step // nj
        for_blocks(cn_ref[cs], lambda b: out_copy(slot, cr_ref[cs], step % nj, b).wait())

    @pl.when(active & (j == 0))
    def _():
        for_blocks(nblk, start_in)
        for_blocks(nblk, wait_in)

    @pl.when(active)
    def _():
        slot = n % 2

        @pl.when(n >= 2)
        def _():
            wait_out(n - 2, slot)

        def compute(row, m):
            if mode == "up":
                if row == 0:
                    w_s[:, :tn] = wg_ref[...].astype(BF16)
                    w_s[:, tn:] = wu_ref[...].astype(BF16)
                bias = jnp.concatenate([bg_ref[...], bu_ref[...]], axis=1)
            else:
                if row == 0:
                    w_s[...] = wd_ref[...].astype(BF16)
                bias = bd_ref[...]
            rows = pl.ds(row, m)
            z = jnp.dot(abuf[rows, :], w_s[...], preferred_element_type=F32) + bias
            if mode == "up":
                g = jnp.minimum(z[:, :tn], SWIGLU_LIMIT)
                u = jnp.clip(z[:, tn:], -SWIGLU_LIMIT, SWIGLU_LIMIT)
                obuf[slot, rows, :] = ((u + 1.0) * (g * jax.nn.sigmoid(SWIGLU_ALPHA * g))).astype(obuf.dtype)
            else:
                rw = rwbuf[rows, :]
                for lc in range(tn // LANE):
                    obuf[slot, rows, lc * LANE:(lc + 1) * LANE] = z[:, lc * LANE:(lc + 1) * LANE] * rw

        for p in range(CHUNK_BLOCKS // 2):
            pl.when(2 * p + 1 < nblk)(functools.partial(compute, 2 * p * mb, 2 * mb))
            pl.when(2 * p + 1 == nblk)(functools.partial(compute, 2 * p * mb, mb))

        for_blocks(nblk, lambda b: out_copy(slot, r0, j, b).start())

        @pl.when(n == last_active)
        def _():
            @pl.when(n >= 1)
            def _():
                wait_out(n - 1, 1 - slot)
            wait_out(n, slot)

    @pl.when((c == n_slots - 1) & (j == nj - 1))
    def _():
        tail0 = meta_ref[1]
        n_tiles = (n_alloc_blocks - tail0) * nj
        obuf[0, 0:mb, :] = jnp.zeros((mb, tn), obuf.dtype)

        def zero_copy(t):
            return pltpu.make_async_copy(
                obuf.at[0, pl.ds(0, mb), :],
                o_hbm.at[pl.ds((tail0 + t // nj) * mb, mb), pl.ds((t % nj) * tn, tn)], zsem)

        def start(t, carry):
            zero_copy(t).start()
            return carry

        def wait(t, carry):
            zero_copy(t).wait()
            return carry

        lax.fori_loop(0, n_tiles, start, 0)
        lax.fori_loop(0, n_tiles, wait, 0)


def _expert_matmul(mode, a, weights, biases, chunks, row_w=None, tn=256):
    chunk_e, chunk_r0, chunk_n, meta = chunks
    n_rows, kdim = a.shape
    n_e, _, ndim = weights[0].shape
    tn = _tile(ndim, tn)
    nj = ndim // tn
    n_slots = chunk_e.shape[0]
    cols = CHUNK_BLOCKS * MOE_BLOCK

    def wmap(c, j, ce, cr, cn, meta):
        return (ce[c], 0, jnp.where(c < meta[0], j, nj - 1))

    w_specs = [pl.BlockSpec((None, kdim, tn), wmap) for _ in weights]
    b_specs = [pl.BlockSpec((None, 1, tn), wmap) for _ in biases]
    any_spec = pl.BlockSpec(memory_space=pl.ANY)
    if mode == "up":
        in_specs = [any_spec] + w_specs + b_specs
        args = [a] + list(weights) + [b.reshape(n_e, 1, ndim) for b in biases]
        out_dtype = BF16
        scratch = [pltpu.VMEM((cols, kdim), BF16), pltpu.VMEM((kdim, 2 * tn), BF16)]
    else:
        in_specs = [any_spec, any_spec] + w_specs + b_specs
        args = [a, row_w] + list(weights) + [b.reshape(n_e, 1, ndim) for b in biases]
        out_dtype = F32
        scratch = [pltpu.VMEM((cols, kdim), BF16), pltpu.VMEM((cols, LANE), F32),
                   pltpu.VMEM((kdim, tn), BF16)]
    scratch += [pltpu.VMEM((2, cols, tn), out_dtype), pltpu.SemaphoreType.DMA(()),
                pltpu.SemaphoreType.DMA((2,)), pltpu.SemaphoreType.DMA(())]
    return pl.pallas_call(
        functools.partial(_expert_chunk_kernel, mode=mode, n_alloc_blocks=n_rows // MOE_BLOCK,
                          n_slots=n_slots, nj=nj),
        grid_spec=pltpu.PrefetchScalarGridSpec(
            num_scalar_prefetch=4,
            grid=(n_slots, nj),
            in_specs=in_specs,
            out_specs=any_spec,
            scratch_shapes=scratch,
        ),
        out_shape=jax.ShapeDtypeStruct((n_rows, ndim), out_dtype),
        compiler_params=_params(("arbitrary", "arbitrary")),
        name="moe_" + mode,
    )(chunk_e, chunk_r0, chunk_n, meta, *args)


def _combine_kernel(pos_ref, h_ref, y_hbm, g_ref, op_ref, os_ref, buf, sem, *, tt, n_prompt_tiles):
    i = pl.program_id(0)
    n = pl.num_programs(0)

    def row_copy(p, slot, k, t):
        return pltpu.make_async_copy(y_hbm.at[pl.ds(p, 1), :], buf.at[slot, k, pl.ds(t, 1), :],
                                     sem.at[slot])

    def start_tile(tile, slot):
        def body(t, carry):
            for k in range(TOP_K):
                row_copy(pos_ref[(tile * tt + t) * TOP_K + k], slot, k, t).start()
            return carry
        lax.fori_loop(0, tt, body, 0, unroll=DMA_ISSUE_UNROLL // TOP_K)

    def wait_tile(slot):
        def body(t, carry):
            for k in range(TOP_K):
                row_copy(0, slot, k, t).wait()
            return carry
        lax.fori_loop(0, tt, body, 0, unroll=DMA_ISSUE_UNROLL // TOP_K)

    @pl.when(i == 0)
    def _():
        start_tile(0, 0)

    @pl.when(i + 1 < n)
    def _():
        start_tile(i + 1, (i + 1) % 2)

    slot = i % 2
    wait_tile(slot)

    def finish(o_ref):
        def rows(c, carry):
            r0 = pl.multiple_of(c * 8, 8)
            acc = h_ref[pl.ds(r0, 8), :]
            for k in range(TOP_K):
                acc = acc + buf[slot, k, pl.ds(r0, 8), :]
            o_ref[pl.ds(r0, 8), :] = _rmsnorm_rows(acc, g_ref[...])
            return carry
        lax.fori_loop(0, tt // 8, rows, 0, unroll=4)

    @pl.when(i < n_prompt_tiles)
    def _():
        finish(op_ref)

    @pl.when(i >= n_prompt_tiles)
    def _():
        finish(os_ref)


def _combine(h1, y, pos, g, tp, tt=64):
    t, d = h1.shape
    ts = t - tp
    tt = _tile_rows(math.gcd(tp, ts), tt)
    npt = tp // tt
    return pl.pallas_call(
        functools.partial(_combine_kernel, tt=tt, n_prompt_tiles=npt),
        grid_spec=pltpu.PrefetchScalarGridSpec(
            num_scalar_prefetch=1,
            grid=(t // tt,),
            in_specs=[
                pl.BlockSpec((tt, d), lambda i, pos: (i, 0)),
                pl.BlockSpec(memory_space=pl.ANY),
                pl.BlockSpec((1, d), lambda i, pos: (0, 0)),
            ],
            out_specs=[
                pl.BlockSpec((tt, d), lambda i, pos: (jnp.minimum(i, npt - 1), 0)),
                pl.BlockSpec((tt, d), lambda i, pos: (jnp.maximum(i - npt, 0), 0)),
            ],
            scratch_shapes=[pltpu.VMEM((2, TOP_K, tt, d), F32), pltpu.SemaphoreType.DMA((2,))],
        ),
        out_shape=[jax.ShapeDtypeStruct((tp, d), F32), jax.ShapeDtypeStruct((ts, d), F32)],
        compiler_params=_params(("arbitrary",)),
        name="moe_combine",
    )(pos, h1, y, g.reshape(1, d))


def _rope_tables(pos):
    half = (HEAD_W // 2) // 8
    inv_freq = ROPE_THETA ** (-jnp.arange(half, dtype=F32) * (2.0 / (2 * half)))
    ang = pos.astype(F32)[:, None] * inv_freq[None, :]
    cos, sin = jnp.cos(ang), jnp.sin(ang)
    ones = jnp.ones((pos.shape[0], HEAD_W // 2 - 2 * half), F32)
    c64 = jnp.concatenate([cos, cos, ones], axis=1)
    s64 = jnp.concatenate([-sin, sin, 0.0 * ones], axis=1)
    return jnp.concatenate([c64, c64], axis=1), jnp.concatenate([s64, s64], axis=1)


def _routing(logits, n_experts, n_blocks):
    t = logits.shape[0]
    top_v, top_i = lax.top_k(logits, TOP_K)
    gate_w = jax.nn.softmax(top_v, axis=-1)
    flat_e = top_i.reshape(-1).astype(jnp.int32)
    flat_t = jnp.repeat(jnp.arange(t, dtype=jnp.int32), TOP_K)
    flat_w = gate_w.reshape(-1)
    order = jnp.argsort(flat_e).astype(jnp.int32)
    slot_of = jnp.argsort(order).astype(jnp.int32)
    experts = jnp.arange(n_experts, dtype=jnp.int32)
    counts = jnp.sum((flat_e[:, None] == experts[None, :]).astype(jnp.int32), axis=0)
    starts = jnp.cumsum(counts) - counts
    padded = (counts + MOE_BLOCK - 1) // MOE_BLOCK * MOE_BLOCK
    padded_end = jnp.cumsum(padded)
    padded_start = padded_end - padded
    n_rows = n_blocks * MOE_BLOCK
    rows = jnp.arange(n_rows, dtype=jnp.int32)
    row_e = jnp.minimum(jnp.sum((rows[:, None] >= padded_end[None, :]).astype(jnp.int32), axis=1),
                        n_experts - 1)
    offset = rows - padded_start[row_e]
    valid = offset < counts[row_e]
    assign = order[jnp.clip(starts[row_e] + offset, 0, t * TOP_K - 1)]
    row_tok = jnp.where(valid, flat_t[assign], 0)
    row_w = jnp.where(valid, flat_w[assign], 0.0)
    pos = padded_start[flat_e] + slot_of - starts[flat_e]
    n_used = (padded_end[-1] // MOE_BLOCK).astype(jnp.int32)
    blocks = padded // MOE_BLOCK
    per_expert = (blocks + CHUNK_BLOCKS - 1) // CHUNK_BLOCKS
    chunk_end = jnp.cumsum(per_expert)
    n_chunks = chunk_end[-1]
    n_slots = -(-n_blocks // CHUNK_BLOCKS) + n_experts
    slots = jnp.arange(n_slots, dtype=jnp.int32)
    s_eff = jnp.minimum(slots, n_chunks - 1)
    chunk_e = jnp.sum((s_eff[:, None] >= chunk_end[None, :]).astype(jnp.int32), axis=1)
    k = s_eff - (chunk_end - per_expert)[chunk_e]
    real = slots < n_chunks
    chunk_r0 = jnp.where(real, padded_start[chunk_e] // MOE_BLOCK + k * CHUNK_BLOCKS, 0)
    chunk_n = jnp.where(real, jnp.clip(blocks[chunk_e] - k * CHUNK_BLOCKS, 0, CHUNK_BLOCKS), 0)
    meta = jnp.stack([n_chunks, n_used]).astype(jnp.int32)
    chunks = (chunk_e.astype(jnp.int32), chunk_r0.astype(jnp.int32), chunk_n.astype(jnp.int32), meta)
    return row_tok, row_w, pos, chunks, n_used.reshape(1)


def kernel(x_prompt, x_sample, cache_fox_k, cache_fox_v, cache_fox_logf, cache_diff_k, cache_diff_v, norm_mix, w_in, b_forget, lambda_q1, lambda_k1, lambda_q2, lambda_k2, diff_subln, w_branch, w_out, norm_ffn, w_router, b_router, moe_w_gate, moe_b_gate, moe_w_up, moe_b_up, moe_w_down, moe_b_down, norm_final):
    batch, seq, d = x_prompt.shape
    dbatch, dseq, _ = x_sample.shape
    depth, _, past, n_heads, _ = cache_fox_k.shape
    assert depth == 1 and cache_fox_k.shape[-1] == HEAD_W and cache_diff_k.shape[-1] == HEAD_W
    assert dseq == CHUNK and past % CHUNK == 0, "sample queries must form exactly the newest chunk"
    mix = n_heads * HEAD_W
    n_experts = w_router.shape[-1]
    tp, ts = batch * seq, dbatch * dseq
    t = tp + ts
    assert (t * TOP_K) % MOE_BLOCK == 0
    xp = x_prompt.reshape(tp, d)
    xs = x_sample.reshape(ts, d)

    off_f = 3 * mix
    off_dq = off_f + n_heads
    off_dv = off_dq + 2 * mix
    off_gate = off_dv + mix
    w = w_in[0]
    w_seg = lambda col0: w[:, col0:col0 + mix].astype(BF16)
    w_fgt = jnp.pad(w[:, off_f:off_dq], ((0, 0), (0, LANE - n_heads))).astype(BF16)
    b_fgt = jnp.pad(b_forget[0], (0, LANE - n_heads)).reshape(1, LANE)
    w_gates = w[:, off_gate:].astype(BF16)

    row_tile = _tile(math.gcd(tp, ts), 1024)
    xn = _norm_pair(xp, xs, norm_mix[0], _tile(math.gcd(tp, ts), 256))

    pos_all = jnp.concatenate([jnp.tile(jnp.arange(seq), batch), jnp.tile(past + jnp.arange(dseq), dbatch)])
    cos_t, sin_t = _rope_tables(pos_all)

    seg = lambda col0, mode, extra, name: _proj(xn, w_seg(col0), mix, mode, extra=extra, tm=row_tile,
                                                name=name, split=tp)
    fq_p, fq_s = seg(0, "plain", (), "proj_fox_q")
    fk_p, fk_s = seg(mix, "plain", (), "proj_fox_k")
    fv_p, fv_s = seg(2 * mix, "plain", (), "proj_fox_v")
    dq_p, dq_s = seg(off_dq, "rope", (cos_t, sin_t), "proj_diff_q")
    dk_p, dk_s = seg(off_dq + mix, "rope", (cos_t, sin_t), "proj_diff_k")
    dv_p, dv_s = seg(off_dv, "plain", (), "proj_diff_v")
    logf = _proj(xn, w_fgt, LANE, "logsig", extra=(b_fgt,), tm=row_tile, name="proj_forget")[:, :n_heads]
    gates = _proj(xn, w_gates, 2 * d, "sigmoid", tm=row_tile, name="proj_gates")

    logf_p = logf[:tp].reshape(batch, seq, n_heads)
    logf_s = logf[tp:].reshape(dbatch, dseq, n_heads)
    ck_p_flat = _cumsum_seq(logf_p)
    ck_s = _cumsum_seq(jnp.concatenate([cache_fox_logf[0].astype(F32), logf_s], axis=1))
    cum_p = ck_p_flat.transpose(0, 2, 1)
    tq = _tile(seq, 512)
    ck_p = ck_p_flat.reshape(batch, n_heads, seq // tq, tq).transpose(0, 2, 1, 3)
    cq_s = ck_s[:, :, past:].transpose(0, 2, 1).reshape(ts, n_heads)

    f32 = F32
    lam_init = 0.8 - 0.6 * math.exp(-0.3 * 0)
    lam = (jnp.exp(jnp.sum(lambda_q1[0].astype(f32) * lambda_k1[0].astype(f32)))
           - jnp.exp(jnp.sum(lambda_q2[0].astype(f32) * lambda_k2[0].astype(f32))) + lam_init).reshape(1)
    subln = diff_subln[0].reshape(1, HEAD_W)
    nh = n_heads
    fox_scale = HEAD_W ** -0.5
    diff_scale = (HEAD_W // 2) ** -0.5

    o_fox_p = _prompt_attention("fox", fq_p, 0, fk_p, 0, fv_p, 0, batch, seq, nh,
                                (cum_p.reshape(tp, nh), ck_p), fox_scale)
    o_diff_p = _prompt_attention("diff", dq_p, 0, dk_p, 0, dv_p, 0, batch, seq, nh,
                                 (lam, subln), diff_scale, post_scale=1.0 - lam_init)
    o_fox_s = _sample_attention("fox", fq_s, 0, cache_fox_k[0], cache_fox_v[0], fk_s, 0, fv_s, 0,
                                0, dbatch, dseq, past, nh,
                                (cq_s, ck_s[:, :, :past], ck_s[:, :, past:]), fox_scale)
    o_diff_s = _sample_attention("diff", dq_s, 0, cache_diff_k[0], cache_diff_v[0], dk_s, 0, dv_s, 0,
                                 0, dbatch, dseq, past, nh, (lam, subln), diff_scale,
                                 post_scale=1.0 - lam_init)

    merged = _merge(o_fox_p, o_fox_s, o_diff_p, o_diff_s, w_branch[0, 0].astype(BF16),
                    w_branch[0, 1].astype(BF16), gates, tm=row_tile)
    h1 = _out_proj(merged, w_out[0].astype(BF16), xp, xs, tm=row_tile)

    w_r = jnp.pad(w_router[0], ((0, 0), (0, LANE - n_experts)))
    b_r = jnp.pad(b_router[0].astype(F32), (0, LANE - n_experts)).reshape(1, LANE)
    logits, hn_packed = _router(h1, norm_ffn[0], w_r, b_r)
    logits = logits[:, :n_experts]
    n_blocks = (t * TOP_K) // MOE_BLOCK + n_experts
    row_tok, row_w, pos, chunks, n_used = _routing(logits, n_experts, n_blocks)
    xs_sorted = _gather_rows(hn_packed, row_tok, n_used, n_blocks)
    hmid = _expert_matmul("up", xs_sorted, [moe_w_gate[0], moe_w_up[0]], [moe_b_gate[0], moe_b_up[0]],
                          chunks, tn=256)
    row_w_b = jnp.broadcast_to(row_w[:, None], (row_w.shape[0], LANE))
    y_rows = _expert_matmul("down", hmid, [moe_w_down[0]], [moe_b_down[0]], chunks, row_w=row_w_b,
                            tn=512)
    y_p, y_s = _combine(h1, y_rows, pos, norm_final, tp)

    def state(arr, b, s):
        return arr.reshape(1, b, s, n_heads, HEAD_W)

    outs_p = (state(fk_p, batch, seq), state(fv_p, batch, seq), logf_p[None],
              state(dk_p, batch, seq), state(dv_p, batch, seq))
    outs_s = (state(fk_s, dbatch, dseq), state(fv_s, dbatch, dseq), logf_s[None],
              state(dk_s, dbatch, dseq), state(dv_s, dbatch, dseq))
    return (y_p.reshape(batch, seq, d), y_s.reshape(dbatch, dseq, d)) + outs_p + outs_s
```

```python
import functools
import math

import jax
import jax.numpy as jnp
from jax import lax
from jax.experimental import pallas as pl
from jax.experimental.pallas import tpu as pltpu

F32 = jnp.float32
BF16 = jnp.bfloat16

CHUNK = 64
ROPE_THETA = 500000.0
TOP_K = 4
SWIGLU_LIMIT = 7.0
SWIGLU_ALPHA = 1.702
MOE_BLOCK = 256
CHUNK_BLOCKS = 8
DMA_ISSUE_UNROLL = 8
NORM_ROWS = 16
NORM_EPS = 1e-5
LANE = 128
HEAD_W = 128
VMEM_LIMIT = 56 * 1024 * 1024
NEG_INIT = -1e30
ATTN_STRIP = 256


def _tile(n, pref):
    if n <= LANE:
        return n
    t = min(pref, n) // LANE * LANE
    while n % t:
        t -= LANE
    return t


def _tile_rows(n, pref):
    t = min(pref, n) // 8 * 8
    while n % t:
        t -= 8
    return t


def _params(sem):
    return pltpu.CompilerParams(dimension_semantics=sem, vmem_limit_bytes=VMEM_LIMIT)


def _rmsnorm_rows(x, g):
    ms = jnp.mean(x * x, axis=-1, keepdims=True)
    return x * lax.rsqrt(ms + NORM_EPS) * g


def _norm_pair_kernel(xp_ref, xs_ref, g_ref, o_ref, *, n_prompt_tiles):
    i = pl.program_id(0)

    @pl.when(i < n_prompt_tiles)
    def _():
        o_ref[...] = _rmsnorm_rows(xp_ref[...], g_ref[...]).astype(o_ref.dtype)

    @pl.when(i >= n_prompt_tiles)
    def _():
        o_ref[...] = _rmsnorm_rows(xs_ref[...], g_ref[...]).astype(o_ref.dtype)


def _norm_pair(xp, xs, g, tm):
    tp, d = xp.shape
    ts = xs.shape[0]
    npt, nst = tp // tm, ts // tm
    return pl.pallas_call(
        functools.partial(_norm_pair_kernel, n_prompt_tiles=npt),
        grid=(npt + nst,),
        in_specs=[
            pl.BlockSpec((tm, d), lambda i: (jnp.minimum(i, npt - 1), 0)),
            pl.BlockSpec((tm, d), lambda i: (jnp.maximum(i - npt, 0), 0)),
            pl.BlockSpec((1, d), lambda i: (0, 0)),
        ],
        out_specs=pl.BlockSpec((tm, d), lambda i: (i, 0)),
        out_shape=jax.ShapeDtypeStruct((tp + ts, d), BF16),
        compiler_params=_params(("arbitrary",)),
        name="norm_mix",
    )(xp, xs, g.reshape(1, d))


def _proj_kernel(a_ref, w_ref, *rest, mode, n_prompt_tiles):
    n_out = 1 if n_prompt_tiles is None else 2
    extra, outs = rest[:len(rest) - n_out], rest[len(rest) - n_out:]
    z = jnp.dot(a_ref[...], w_ref[...], preferred_element_type=F32)

    def store(o_ref):
        if mode == "plain":
            o_ref[...] = z
        elif mode == "sigmoid":
            o_ref[...] = jax.nn.sigmoid(z)
        elif mode == "logsig":
            (b_ref,) = extra
            o_ref[...] = jax.nn.log_sigmoid(z + b_ref[...])
        else:
            c_ref, s_ref = extra
            tm, tn = z.shape
            lane = lax.broadcasted_iota(jnp.int32, (tm, LANE), 1)
            first_half = (lane % 64) < 8
            cos, sin = c_ref[...], s_ref[...]
            for c in range(tn // LANE):
                zc = z[:, c * LANE:(c + 1) * LANE]
                rot = jnp.where(first_half, pltpu.roll(zc, LANE - 8, 1), pltpu.roll(zc, 8, 1))
                o_ref[:, c * LANE:(c + 1) * LANE] = zc * cos + rot * sin

    if n_prompt_tiles is None:
        store(outs[0])
    else:
        i = pl.program_id(0)
        pl.when(i < n_prompt_tiles)(functools.partial(store, outs[0]))
        pl.when(i >= n_prompt_tiles)(functools.partial(store, outs[1]))


def _proj(a, w, n_cols, mode, extra=(), tm=1024, tn=512, name="proj", split=None):
    t, d = a.shape
    tm = _tile(t, tm)
    tn = _tile(n_cols, tn)
    nj = n_cols // tn
    in_specs = [
        pl.BlockSpec((tm, d), lambda i, j: (i, 0)),
        pl.BlockSpec((d, tn), lambda i, j: (0, j)),
    ]
    if mode == "logsig":
        in_specs.append(pl.BlockSpec((1, tn), lambda i, j: (0, j)))
    elif mode == "rope":
        in_specs += [pl.BlockSpec((tm, LANE), lambda i, j: (i, 0))] * 2
    if split is None:
        npt = None
        out_specs = pl.BlockSpec((tm, tn), lambda i, j: (i, j))
        out_shape = jax.ShapeDtypeStruct((t, n_cols), F32)
    else:
        assert split % tm == 0
        npt = split // tm
        out_specs = [
            pl.BlockSpec((tm, tn), lambda i, j: (jnp.minimum(i, npt - 1), jnp.where(i < npt, j, nj - 1))),
            pl.BlockSpec((tm, tn), lambda i, j: (jnp.maximum(i - npt, 0), jnp.where(i >= npt, j, 0))),
        ]
        out_shape = [jax.ShapeDtypeStruct((split, n_cols), F32),
                     jax.ShapeDtypeStruct((t - split, n_cols), F32)]
    return pl.pallas_call(
        functools.partial(_proj_kernel, mode=mode, n_prompt_tiles=npt),
        grid=(t // tm, nj),
        in_specs=in_specs,
        out_specs=out_specs,
        out_shape=out_shape,
        compiler_params=_params(("arbitrary", "arbitrary")),
        name=name,
    )(a, w, *extra)


def _cumsum_kernel(x_ref, o_ref):
    rows, w = x_ref.shape
    lane = lax.broadcasted_iota(jnp.int32, (rows, LANE), 1)
    total = jnp.zeros((rows, 1), F32)
    for c in range(w // LANE):
        v = x_ref[:, c * LANE:(c + 1) * LANE]
        shift = 1
        while shift < LANE:
            v = v + jnp.where(lane >= shift, pltpu.roll(v, shift, 1), 0.0)
            shift *= 2
        v = v + total
        o_ref[:, c * LANE:(c + 1) * LANE] = v
        total = v[:, LANE - 1:LANE]


def _cumsum_seq(x):
    b, s, h = x.shape
    w = -(-s // LANE) * LANE
    xt = jnp.pad(x.transpose(0, 2, 1), ((0, 0), (0, 0), (0, w - s)))
    out = pl.pallas_call(
        _cumsum_kernel,
        grid=(b,),
        in_specs=[pl.BlockSpec((None, h, w), lambda i: (i, 0, 0))],
        out_specs=pl.BlockSpec((None, h, w), lambda i: (i, 0, 0)),
        out_shape=jax.ShapeDtypeStruct((b, h, w), F32),
        compiler_params=_params(("arbitrary",)),
        name="cumsum_logf",
    )(xt)
    return out[:, :, :s]


def _qk(q, k):
    return lax.dot_general(q, k, (((1,), (1,)), ((), ())), preferred_element_type=F32)


def _pick_lane(x, idx):
    lane = lax.broadcasted_iota(jnp.int32, x.shape, 1)
    return jnp.sum(jnp.where(lane == idx, x, 0.0), axis=1, keepdims=True)


def _diff_finish(o1, o2, lam, g, post_scale):
    o = o1 - lam * o2
    return _rmsnorm_rows(o, g) * post_scale


def _prompt_attn_kernel(*refs, mode, tq, rs, hb, scale, post_scale):
    if mode == "fox":
        q_ref, k_ref, v_ref, cq_ref, ck_ref, o_ref, q_s, kT_s, v_s, m_s, acc_s = refs
    else:
        lam_ref, q_ref, k_ref, v_ref, g_ref, o_ref, q_s, kT_s, v_s, m_s, acc_s = refs
    hg = pl.program_id(1)
    qi = pl.program_id(2)
    nk = kT_s.shape[1]
    seq = v_s.shape[1]
    maps_per_head = 1 if mode == "fox" else 2

    @pl.when(qi == 0)
    def _():
        for hh in range(hb):
            sl = slice(hh * HEAD_W, (hh + 1) * HEAD_W)
            v_s[hh, :, :HEAD_W] = v_ref[:, sl].astype(BF16)
            v_s[hh, :, HEAD_W:] = jnp.ones((seq, HEAD_W), BF16)
            for kj in range(nk):
                kT_s[hh, kj] = k_ref[kj * tq:(kj + 1) * tq, sl].T.astype(BF16)

    m_s[...] = jnp.full(m_s.shape, NEG_INIT, F32)
    acc_s[...] = jnp.zeros(acc_s.shape, F32)

    cqs = []
    for hh in range(hb):
        q = q_ref[:, hh * HEAD_W:(hh + 1) * HEAD_W] * scale
        if mode == "fox":
            q_s[hh] = q.astype(BF16)
            cqs.append(_pick_lane(cq_ref[...], hg * hb + hh))
        else:
            lane = lax.broadcasted_iota(jnp.int32, q.shape, 1)
            q_s[2 * hh] = jnp.where(lane < HEAD_W // 2, q, 0.0).astype(BF16)
            q_s[2 * hh + 1] = jnp.where(lane >= HEAD_W // 2, q, 0.0).astype(BF16)

    def tile(kj, diagonal):
        ks = pl.multiple_of(kj * tq, tq)
        for r0 in range(0, tq, rs):
            ncol = r0 + rs if diagonal else tq
            if diagonal:
                r = r0 + lax.broadcasted_iota(jnp.int32, (rs, ncol), 0)
                c = lax.broadcasted_iota(jnp.int32, (rs, ncol), 1)
                visible = (c <= r) if mode == "fox" else ((c // CHUNK) <= (r // CHUNK))
            for mi in range(hb * maps_per_head):
                hh = mi // maps_per_head
                s = jnp.dot(q_s[mi, r0:r0 + rs, :], kT_s[hh, kj, :, :ncol], preferred_element_type=F32)
                if mode == "fox":
                    ck = ck_ref[kj, pl.ds(hg * hb + hh, 1), :]
                    s = s + (cqs[hh][r0:r0 + rs] - ck[:, :ncol])
                if diagonal:
                    s = jnp.where(visible, s, -jnp.inf)
                m_prev = m_s[mi, r0:r0 + rs, :]
                m_new = jnp.maximum(m_prev, jnp.max(s, axis=-1, keepdims=True))
                alpha = jnp.exp(m_prev - m_new)
                p = jnp.exp(s - m_new[:, :1])
                pv = jnp.dot(p.astype(BF16), v_s[hh, pl.ds(ks, ncol), :], preferred_element_type=F32)
                acc_s[mi, r0:r0 + rs, :] = (jnp.concatenate([alpha, alpha], axis=1) * acc_s[mi, r0:r0 + rs, :]
                                            + pv)
                m_s[mi, r0:r0 + rs, :] = m_new

    def body(kj, carry):
        tile(kj, False)
        return carry

    lax.fori_loop(0, qi, body, 0)
    tile(qi, True)

    for hh in range(hb):
        sl = slice(hh * HEAD_W, (hh + 1) * HEAD_W)
        if mode == "fox":
            a = acc_s[hh]
            o = a[:, :HEAD_W] / a[:, HEAD_W:]
        else:
            a1, a2 = acc_s[2 * hh], acc_s[2 * hh + 1]
            o = _diff_finish(a1[:, :HEAD_W] / a1[:, HEAD_W:], a2[:, :HEAD_W] / a2[:, HEAD_W:],
                             lam_ref[0], g_ref[...], post_scale)
        o_ref[:, sl] = o.astype(o_ref.dtype)


def _prompt_attention(mode, q_arr, q_col0, k_arr, k_col0, v_arr, v_col0, batch, seq, n_heads,
                      extra, scale, post_scale=None):
    tq = _tile(seq, 512)
    nq = seq // tq
    hb = 4 if n_heads % 4 == 0 else (2 if n_heads % 2 == 0 else 1)
    w = hb * HEAD_W
    n_maps = hb if mode == "fox" else 2 * hb
    row_q = lambda b, hg, qi: (b * nq + qi, q_col0 // hb + hg)
    kv_spec = lambda col0: pl.BlockSpec((seq, w), lambda b, hg, qi: (b, col0 // hb + hg))
    if mode == "fox":
        cq, ck = extra
        in_specs = [
            pl.BlockSpec((tq, w), row_q), kv_spec(k_col0), kv_spec(v_col0),
            pl.BlockSpec((tq, n_heads), lambda b, hg, qi: (b * nq + qi, 0)),
            pl.BlockSpec((None, nq, n_heads, tq), lambda b, hg, qi: (b, 0, 0, 0)),
        ]
        args = (q_arr, k_arr, v_arr, cq, ck)
    else:
        lam, g = extra
        in_specs = [
            pl.BlockSpec(memory_space=pltpu.SMEM),
            pl.BlockSpec((tq, w), row_q), kv_spec(k_col0), kv_spec(v_col0),
            pl.BlockSpec((1, HEAD_W), lambda b, hg, qi: (0, 0)),
        ]
        args = (lam, q_arr, k_arr, v_arr, g)
    return pl.pallas_call(
        functools.partial(_prompt_attn_kernel, mode=mode, tq=tq, rs=min(ATTN_STRIP, tq), hb=hb,
                          scale=scale, post_scale=post_scale),
        grid=(batch, n_heads // hb, nq),
        in_specs=in_specs,
        out_specs=pl.BlockSpec((tq, w), lambda b, hg, qi: (b * nq + qi, hg)),
        out_shape=jax.ShapeDtypeStruct((batch * seq, n_heads * HEAD_W), BF16),
        scratch_shapes=[
            pltpu.VMEM((n_maps, tq, HEAD_W), BF16),
            pltpu.VMEM((hb, nq, HEAD_W, tq), BF16),
            pltpu.VMEM((hb, seq, 2 * HEAD_W), BF16),
            pltpu.VMEM((n_maps, tq, LANE), F32),
            pltpu.VMEM((n_maps, tq, 2 * HEAD_W), F32),
        ],
        compiler_params=_params(("arbitrary", "arbitrary", "arbitrary")),
        name=mode + "_prompt_attn",
    )(*args)


def _cache_fetch(kc_hbm, vc_hbm, kbuf, vbuf, sem, heads_per_step, n_groups, n_steps):
    n = pl.program_id(0) * n_groups + pl.program_id(1)

    def copies(step, slot):
        b = step // n_groups
        h0 = (step % n_groups) * heads_per_step
        cps = []
        for hh in range(heads_per_step):
            cps.append(pltpu.make_async_copy(kc_hbm.at[b, :, h0 + hh, :], kbuf.at[slot, hh], sem.at[slot]))
            cps.append(pltpu.make_async_copy(vc_hbm.at[b, :, h0 + hh, :], vbuf.at[slot, hh], sem.at[slot]))
        return cps

    @pl.when(n == 0)
    def _():
        for cp in copies(0, 0):
            cp.start()

    @pl.when(n + 1 < n_steps)
    def _():
        for cp in copies(n + 1, (n + 1) % 2):
            cp.start()

    slot = n % 2
    for cp in copies(n, slot):
        cp.wait()
    return slot


def _fox_sample_kernel(q_ref, kc_hbm, vc_hbm, kn_ref, vn_ref, cq_ref, ckc_ref, ckn_ref, o_ref,
                       kbuf, vbuf, sem, *, heads_per_step, n_groups, n_steps, scale):
    hg = pl.program_id(1)
    slot = _cache_fetch(kc_hbm, vc_hbm, kbuf, vbuf, sem, heads_per_step, n_groups, n_steps)
    for hh in range(heads_per_step):
        sl = slice(hh * HEAD_W, (hh + 1) * HEAD_W)
        h = hg * heads_per_step + hh
        q = (q_ref[:, sl] * scale).astype(BF16)
        cq = _pick_lane(cq_ref[...], h)
        sc = _qk(q, kbuf[slot, hh].astype(BF16)) + cq - ckc_ref[pl.ds(h, 1), :]
        sn = _qk(q, kn_ref[:, sl].astype(BF16)) + cq - ckn_ref[pl.ds(h, 1), :]
        r = lax.broadcasted_iota(jnp.int32, sn.shape, 0)
        c = lax.broadcasted_iota(jnp.int32, sn.shape, 1)
        sn = jnp.where(c <= r, sn, -jnp.inf)
        m = jnp.maximum(jnp.max(sc, axis=-1, keepdims=True), jnp.max(sn, axis=-1, keepdims=True))
        pc = jnp.exp(sc - m)
        pn = jnp.exp(sn - m)
        l = jnp.sum(pc, axis=-1, keepdims=True) + jnp.sum(pn, axis=-1, keepdims=True)
        o = (jnp.dot(pc.astype(BF16), vbuf[slot, hh].astype(BF16), preferred_element_type=F32)
             + jnp.dot(pn.astype(BF16), vn_ref[:, sl].astype(BF16), preferred_element_type=F32))
        o_ref[:, sl] = (o / l).astype(o_ref.dtype)


def _diff_sample_kernel(lam_ref, q_ref, kc_hbm, vc_hbm, kn_ref, vn_ref, g_ref, o_ref,
                        kbuf, vbuf, sem, *, heads_per_step, n_groups, n_steps, scale, post_scale):
    slot = _cache_fetch(kc_hbm, vc_hbm, kbuf, vbuf, sem, heads_per_step, n_groups, n_steps)
    for hh in range(heads_per_step):
        sl = slice(hh * HEAD_W, (hh + 1) * HEAD_W)
        q = q_ref[:, sl] * scale
        lane = lax.broadcasted_iota(jnp.int32, q.shape, 1)
        kc = kbuf[slot, hh].astype(BF16)
        kn = kn_ref[:, sl].astype(BF16)
        vc = vbuf[slot, hh].astype(BF16)
        vn = vn_ref[:, sl].astype(BF16)
        outs = []
        for keep in (lane < HEAD_W // 2, lane >= HEAD_W // 2):
            qh = jnp.where(keep, q, 0.0).astype(BF16)
            sc = _qk(qh, kc)
            sn = _qk(qh, kn)
            m = jnp.maximum(jnp.max(sc, axis=-1, keepdims=True), jnp.max(sn, axis=-1, keepdims=True))
            pc = jnp.exp(sc - m)
            pn = jnp.exp(sn - m)
            l = jnp.sum(pc, axis=-1, keepdims=True) + jnp.sum(pn, axis=-1, keepdims=True)
            o = (jnp.dot(pc.astype(BF16), vc, preferred_element_type=F32)
                 + jnp.dot(pn.astype(BF16), vn, preferred_element_type=F32))
            outs.append(o / l)
        o = _diff_finish(outs[0], outs[1], lam_ref[0], g_ref[...], post_scale)
        o_ref[:, sl] = o.astype(o_ref.dtype)


def _sample_attention(mode, q_arr, q_col0, kc_arr, vc_arr, kn_arr, kn_col0, vn_arr, vn_col0,
                      row0, batch, seq, past, n_heads, extra, scale, post_scale=None):
    hb = 4 if n_heads % 4 == 0 else 1
    w = hb * HEAD_W
    rb0 = row0 // seq
    n_groups = n_heads // hb
    ring = dict(heads_per_step=hb, n_groups=n_groups, n_steps=batch * n_groups)
    new = lambda col0: pl.BlockSpec((seq, w), lambda b, hg: (rb0 + b, col0 // hb + hg))
    cache = pl.BlockSpec(memory_space=pl.ANY)
    if mode == "fox":
        cq, ckc, ckn = extra
        kern = functools.partial(_fox_sample_kernel, scale=scale, **ring)
        in_specs = [
            new(q_col0), cache, cache, new(kn_col0), new(vn_col0),
            pl.BlockSpec((seq, n_heads), lambda b, hg: (b, 0)),
            pl.BlockSpec((None, n_heads, past), lambda b, hg: (b, 0, 0)),
            pl.BlockSpec((None, n_heads, seq), lambda b, hg: (b, 0, 0)),
        ]
        args = (q_arr, kc_arr, vc_arr, kn_arr, vn_arr, cq, ckc, ckn)
    else:
        lam, g = extra
        kern = functools.partial(_diff_sample_kernel, scale=scale, post_scale=post_scale, **ring)
        in_specs = [
            pl.BlockSpec(memory_space=pltpu.SMEM),
            new(q_col0), cache, cache, new(kn_col0), new(vn_col0),
            pl.BlockSpec((1, HEAD_W), lambda b, hg: (0, 0)),
        ]
        args = (lam, q_arr, kc_arr, vc_arr, kn_arr, vn_arr, g)
    return pl.pallas_call(
        kern,
        grid=(batch, n_groups),
        in_specs=in_specs,
        out_specs=pl.BlockSpec((seq, w), lambda b, hg: (b, hg)),
        out_shape=jax.ShapeDtypeStruct((batch * seq, n_heads * HEAD_W), BF16),
        scratch_shapes=[pltpu.VMEM((2, hb, past, HEAD_W), F32), pltpu.VMEM((2, hb, past, HEAD_W), F32),
                        pltpu.SemaphoreType.DMA((2,))],
        compiler_params=_params(("arbitrary", "arbitrary")),
        name=mode + "_sample_attn",
    )(*args)


def _merge_kernel(ofp_ref, ofs_ref, odp_ref, ods_ref, wf_ref, wd_ref, gf_ref, gd_ref, o_ref,
                  *, n_prompt_tiles):
    i = pl.program_id(0)

    def run(of_ref, od_ref):
        a = jnp.dot(of_ref[...], wf_ref[...], preferred_element_type=F32)
        b = jnp.dot(od_ref[...], wd_ref[...], preferred_element_type=F32)
        o_ref[...] = (gf_ref[...] * a + gd_ref[...] * b).astype(o_ref.dtype)

    pl.when(i < n_prompt_tiles)(functools.partial(run, ofp_ref, odp_ref))
    pl.when(i >= n_prompt_tiles)(functools.partial(run, ofs_ref, ods_ref))


def _merge(o_fox_p, o_fox_s, o_diff_p, o_diff_s, wb_fox, wb_diff, gates, tm=1024, tn=512):
    tp, mix = o_fox_p.shape
    ts = o_fox_s.shape[0]
    d = wb_fox.shape[1]
    tm, tn = _tile(math.gcd(tp, ts), tm), _tile(d, tn)
    nj = d // tn
    npt = tp // tm
    prompt = pl.BlockSpec((tm, mix), lambda i, j: (jnp.minimum(i, npt - 1), 0))
    sample = pl.BlockSpec((tm, mix), lambda i, j: (jnp.maximum(i - npt, 0), 0))
    return pl.pallas_call(
        functools.partial(_merge_kernel, n_prompt_tiles=npt),
        grid=((tp + ts) // tm, nj),
        in_specs=[
            prompt, sample, prompt, sample,
            pl.BlockSpec((mix, tn), lambda i, j: (0, j)),
            pl.BlockSpec((mix, tn), lambda i, j: (0, j)),
            pl.BlockSpec((tm, tn), lambda i, j: (i, j)),
            pl.BlockSpec((tm, tn), lambda i, j: (i, nj + j)),
        ],
        out_specs=pl.BlockSpec((tm, tn), lambda i, j: (i, j)),
        out_shape=jax.ShapeDtypeStruct((tp + ts, d), BF16),
        compiler_params=_params(("arbitrary", "arbitrary")),
        name="branch_merge",
    )(o_fox_p, o_fox_s, o_diff_p, o_diff_s, wb_fox, wb_diff, gates, gates)


def _out_proj_kernel(a_ref, w_ref, xp_ref, xs_ref, o_ref, *, n_prompt_tiles):
    i = pl.program_id(0)
    z = jnp.dot(a_ref[...], w_ref[...], preferred_element_type=F32)

    @pl.when(i < n_prompt_tiles)
    def _():
        o_ref[...] = xp_ref[...] + z

    @pl.when(i >= n_prompt_tiles)
    def _():
        o_ref[...] = xs_ref[...] + z


def _out_proj(merged, w_out, xp, xs, tm=1024, tn=512):
    t, d = merged.shape
    tp, ts = xp.shape[0], xs.shape[0]
    tm = _tile(math.gcd(tp, ts), tm)
    tn = _tile(d, tn)
    npt = tp // tm
    return pl.pallas_call(
        functools.partial(_out_proj_kernel, n_prompt_tiles=npt),
        grid=(t // tm, d // tn),
        in_specs=[
            pl.BlockSpec((tm, d), lambda i, j: (i, 0)),
            pl.BlockSpec((d, tn), lambda i, j: (0, j)),
            pl.BlockSpec((tm, tn), lambda i, j: (jnp.minimum(i, npt - 1), j)),
            pl.BlockSpec((tm, tn), lambda i, j: (jnp.maximum(i - npt, 0), j)),
        ],
        out_specs=pl.BlockSpec((tm, tn), lambda i, j: (i, j)),
        out_shape=jax.ShapeDtypeStruct((t, d), F32),
        compiler_params=_params(("arbitrary", "arbitrary")),
        name="out_proj",
    )(merged, w_out, xp, xs)


def _router_kernel(h_ref, g_ref, w_ref, b_ref, o_ref, p_ref):
    hn = _rmsnorm_rows(h_ref[...], g_ref[...])
    o_ref[...] = jnp.dot(hn, w_ref[...], preferred_element_type=F32,
                         precision=lax.Precision.HIGHEST) + b_ref[...]
    half = hn.shape[1] // 2
    lo = lax.bitcast_convert_type(hn[:, :half].astype(BF16).astype(F32), jnp.uint32)
    hi = lax.bitcast_convert_type(hn[:, half:].astype(BF16).astype(F32), jnp.uint32)
    p_ref[...] = (lo >> 16) | (hi & jnp.uint32(0xFFFF0000))


def _router(h1, g, w_pad, b_pad, tm=256):
    t, d = h1.shape
    tm = _tile(t, tm)
    n = w_pad.shape[1]
    return pl.pallas_call(
        _router_kernel,
        grid=(t // tm,),
        in_specs=[
            pl.BlockSpec((tm, d), lambda i: (i, 0)),
            pl.BlockSpec((1, d), lambda i: (0, 0)),
            pl.BlockSpec((d, n), lambda i: (0, 0)),
            pl.BlockSpec((1, n), lambda i: (0, 0)),
        ],
        out_specs=[pl.BlockSpec((tm, n), lambda i: (i, 0)), pl.BlockSpec((tm, d // 2), lambda i: (i, 0))],
        out_shape=[jax.ShapeDtypeStruct((t, n), F32), jax.ShapeDtypeStruct((t, d // 2), jnp.uint32)],
        compiler_params=_params(("arbitrary",)),
        name="router",
    )(h1, g.reshape(1, d), w_pad, b_pad)


def _gather_kernel(tok_ref, nu_ref, h_hbm, o_ref, buf, sem):
    i = pl.program_id(0)
    n_used = nu_ref[0]

    def row_copy(tok, slot, r):
        return pltpu.make_async_copy(h_hbm.at[pl.ds(tok, 1), :], buf.at[slot, pl.ds(r, 1), :],
                                     sem.at[slot])

    def start_block(blk, slot):
        def body(r2, carry):
            for u in range(2):
                r = r2 * 2 + u
                row_copy(tok_ref[blk * MOE_BLOCK + r], slot, r).start(priority=u)
            return carry
        lax.fori_loop(0, MOE_BLOCK // 2, body, 0, unroll=DMA_ISSUE_UNROLL // 2)

    def wait_block(slot):
        def body(r, carry):
            row_copy(0, slot, r).wait()
            return carry
        lax.fori_loop(0, MOE_BLOCK, body, 0, unroll=DMA_ISSUE_UNROLL)

    @pl.when(i == 0)
    def _():
        start_block(0, 0)

    @pl.when(i + 1 < n_used)
    def _():
        start_block(i + 1, (i + 1) % 2)

    @pl.when(i < n_used)
    def _():
        slot = i % 2
        wait_block(slot)

        half = buf.shape[-1]

        def unpack_rows(c, carry):
            r0 = pl.multiple_of(c * NORM_ROWS, NORM_ROWS)
            w = buf[slot, pl.ds(r0, NORM_ROWS), :]
            lo = lax.bitcast_convert_type(w << 16, F32)
            hi = lax.bitcast_convert_type(w & jnp.uint32(0xFFFF0000), F32)
            o_ref[pl.ds(r0, NORM_ROWS), :half] = lo.astype(o_ref.dtype)
            o_ref[pl.ds(r0, NORM_ROWS), half:] = hi.astype(o_ref.dtype)
            return carry
        lax.fori_loop(0, MOE_BLOCK // NORM_ROWS, unpack_rows, 0, unroll=4)

    @pl.when(i >= n_used)
    def _():
        o_ref[...] = jnp.zeros(o_ref.shape, o_ref.dtype)


def _gather_rows(hn_packed, row_tok, n_used, n_blocks):
    half = hn_packed.shape[1]
    return pl.pallas_call(
        _gather_kernel,
        grid_spec=pltpu.PrefetchScalarGridSpec(
            num_scalar_prefetch=2,
            grid=(n_blocks,),
            in_specs=[pl.BlockSpec(memory_space=pl.ANY)],
            out_specs=pl.BlockSpec((MOE_BLOCK, 2 * half), lambda i, tok, nu: (i, 0)),
            scratch_shapes=[pltpu.VMEM((2, MOE_BLOCK, half), jnp.uint32), pltpu.SemaphoreType.DMA((2,))],
        ),
        out_shape=jax.ShapeDtypeStruct((n_blocks * MOE_BLOCK, 2 * half), BF16),
        compiler_params=_params(("arbitrary",)),
        name="moe_gather",
    )(row_tok, n_used, hn_packed)


def _expert_chunk_kernel(ce_ref, cr_ref, cn_ref, meta_ref, *refs, mode, n_alloc_blocks, n_slots, nj):
    if mode == "up":
        a_hbm, wg_ref, wu_ref, bg_ref, bu_ref, o_hbm, abuf, w_s, obuf, isem, osem, zsem = refs
    else:
        a_hbm, rw_hbm, wd_ref, bd_ref, o_hbm, abuf, rwbuf, w_s, obuf, isem, osem, zsem = refs
    c = pl.program_id(0)
    j = pl.program_id(1)
    tn = obuf.shape[-1]
    mb = MOE_BLOCK
    n_chunks = meta_ref[0]
    n = c * nj + j
    last_active = n_chunks * nj - 1
    active = c < n_chunks
    nblk = cn_ref[c]
    r0 = cr_ref[c]

    def in_copies(b):
        rows_hbm = pl.ds((r0 + b) * mb, mb)
        rows_buf = pl.ds(b * mb, mb)
        cps = [pltpu.make_async_copy(a_hbm.at[rows_hbm, :], abuf.at[rows_buf, :], isem)]
        if mode == "down":
            cps.append(pltpu.make_async_copy(rw_hbm.at[rows_hbm, :], rwbuf.at[rows_buf, :], isem))
        return cps

    def out_copy(slot, row_blk, col, b):
        return pltpu.make_async_copy(
            obuf.at[slot, pl.ds(b * mb, mb), :],
            o_hbm.at[pl.ds((row_blk + b) * mb, mb), pl.ds(col * tn, tn)], osem.at[slot])

    def for_blocks(count, fn):
        for b in range(CHUNK_BLOCKS):
            pl.when(b < count)(functools.partial(fn, b))

    def start_in(b):
        for cp in in_copies(b):
            cp.start()

    def wait_in(b):
        for cp in in_copies(b):
            cp.wait()

    def wait_out(step, slot):
        cs = step // nj
        for_blocks(cn_ref[cs], lambda b: out_copy(slot, cr_ref[cs], step % nj, b).wait())

    @pl.when(active & (j == 0))
    def _():
        for_blocks(nblk, start_in)
        for_blocks(nblk, wait_in)

    @pl.when(active)
    def _():
        slot = n % 2

        @pl.when(n >= 2)
        def _():
            wait_out(n - 2, slot)

        def compute(row, m):
            if mode == "up":
                if row == 0:
                    w_s[:, :tn] = wg_ref[...].astype(BF16)
                    w_s[:, tn:] = wu_ref[...].astype(BF16)
                bias = jnp.concatenate([bg_ref[...], bu_ref[...]], axis=1)
            else:
                if row == 0:
                    w_s[...] = wd_ref[...].astype(BF16)
                bias = bd_ref[...]
            rows = pl.ds(row, m)
            z = jnp.dot(abuf[rows, :], w_s[...], preferred_element_type=F32) + bias
            if mode == "up":
                g = jnp.minimum(z[:, :tn], SWIGLU_LIMIT)
                u = jnp.clip(z[:, tn:], -SWIGLU_LIMIT, SWIGLU_LIMIT)
                obuf[slot, rows, :] = ((u + 1.0) * (g * jax.nn.sigmoid(SWIGLU_ALPHA * g))).astype(obuf.dtype)
            else:
                rw = rwbuf[rows, :]
                for lc in range(tn // LANE):
                    obuf[slot, rows, lc * LANE:(lc + 1) * LANE] = z[:, lc * LANE:(lc + 1) * LANE] * rw

        for p in range(CHUNK_BLOCKS // 2):
            pl.when(2 * p + 1 < nblk)(functools.partial(compute, 2 * p * mb, 2 * mb))
            pl.when(2 * p + 1 == nblk)(functools.partial(compute, 2 * p * mb, mb))

        for_blocks(nblk, lambda b: out_copy(slot, r0, j, b).start())

        @pl.when(n == last_active)
        def _():
            @pl.when(n >= 1)
            def _():
                wait_out(n - 1, 1 - slot)
            wait_out(n, slot)

    @pl.when((c == n_slots - 1) & (j == nj - 1))
    def _():
        tail0 = meta_ref[1]
        n_tiles = (n_alloc_blocks - tail0) * nj
        obuf[0, 0:mb, :] = jnp.zeros((mb, tn), obuf.dtype)

        def zero_copy(t):
            return pltpu.make_async_copy(
                obuf.at[0, pl.ds(0, mb), :],
                o_hbm.at[pl.ds((tail0 + t // nj) * mb, mb), pl.ds((t % nj) * tn, tn)], zsem)

        def start(t, carry):
            zero_copy(t).start()
            return carry

        def wait(t, carry):
            zero_copy(t).wait()
            return carry

        lax.fori_loop(0, n_tiles, start, 0)
        lax.fori_loop(0, n_tiles, wait, 0)


def _expert_matmul(mode, a, weights, biases, chunks, row_w=None, tn=256):
    chunk_e, chunk_r0, chunk_n, meta = chunks
    n_rows, kdim = a.shape
    n_e, _, ndim = weights[0].shape
    tn = _tile(ndim, tn)
    nj = ndim // tn
    n_slots = chunk_e.shape[0]
    cols = CHUNK_BLOCKS * MOE_BLOCK

    def wmap(c, j, ce, cr, cn, meta):
        return (ce[c], 0, jnp.where(c < meta[0], j, nj - 1))

    w_specs = [pl.BlockSpec((None, kdim, tn), wmap) for _ in weights]
    b_specs = [pl.BlockSpec((None, 1, tn), wmap) for _ in biases]
    any_spec = pl.BlockSpec(memory_space=pl.ANY)
    if mode == "up":
        in_specs = [any_spec] + w_specs + b_specs
        args = [a] + list(weights) + [b.reshape(n_e, 1, ndim) for b in biases]
        out_dtype = BF16
        scratch = [pltpu.VMEM((cols, kdim), BF16), pltpu.VMEM((kdim, 2 * tn), BF16)]
    else:
        in_specs = [any_spec, any_spec] + w_specs + b_specs
        args = [a, row_w] + list(weights) + [b.reshape(n_e, 1, ndim) for b in biases]
        out_dtype = F32
        scratch = [pltpu.VMEM((cols, kdim), BF16), pltpu.VMEM((cols, LANE), F32),
                   pltpu.VMEM((kdim, tn), BF16)]
    scratch += [pltpu.VMEM((2, cols, tn), out_dtype), pltpu.SemaphoreType.DMA(()),
                pltpu.SemaphoreType.DMA((2,)), pltpu.SemaphoreType.DMA(())]
    return pl.pallas_call(
        functools.partial(_expert_chunk_kernel, mode=mode, n_alloc_blocks=n_rows // MOE_BLOCK,
                          n_slots=n_slots, nj=nj),
        grid_spec=pltpu.PrefetchScalarGridSpec(
            num_scalar_prefetch=4,
            grid=(n_slots, nj),
            in_specs=in_specs,
            out_specs=any_spec,
            scratch_shapes=scratch,
        ),
        out_shape=jax.ShapeDtypeStruct((n_rows, ndim), out_dtype),
        compiler_params=_params(("arbitrary", "arbitrary")),
        name="moe_" + mode,
    )(chunk_e, chunk_r0, chunk_n, meta, *args)


def _combine_kernel(pos_ref, h_ref, y_hbm, g_ref, op_ref, os_ref, buf, sem, *, tt, n_prompt_tiles):
    i = pl.program_id(0)
    n = pl.num_programs(0)

    def row_copy(p, slot, k, t):
        return pltpu.make_async_copy(y_hbm.at[pl.ds(p, 1), :], buf.at[slot, k, pl.ds(t, 1), :],
                                     sem.at[slot])

    def start_tile(tile, slot):
        def body(t, carry):
            for k in range(TOP_K):
                row_copy(pos_ref[(tile * tt + t) * TOP_K + k], slot, k, t).start(priority=k % 2)
            return carry
        lax.fori_loop(0, tt, body, 0, unroll=DMA_ISSUE_UNROLL // TOP_K)

    def wait_tile(slot):
        def body(t, carry):
            for k in range(TOP_K):
                row_copy(0, slot, k, t).wait()
            return carry
        lax.fori_loop(0, tt, body, 0, unroll=DMA_ISSUE_UNROLL // TOP_K)

    @pl.when(i == 0)
    def _():
        start_tile(0, 0)

    @pl.when(i + 1 < n)
    def _():
        start_tile(i + 1, (i + 1) % 2)

    slot = i % 2
    wait_tile(slot)

    def finish(o_ref):
        def rows(c, carry):
            r0 = pl.multiple_of(c * 8, 8)
            acc = h_ref[pl.ds(r0, 8), :]
            for k in range(TOP_K):
                acc = acc + buf[slot, k, pl.ds(r0, 8), :]
            o_ref[pl.ds(r0, 8), :] = _rmsnorm_rows(acc, g_ref[...])
            return carry
        lax.fori_loop(0, tt // 8, rows, 0, unroll=4)

    @pl.when(i < n_prompt_tiles)
    def _():
        finish(op_ref)

    @pl.when(i >= n_prompt_tiles)
    def _():
        finish(os_ref)


def _combine(h1, y, pos, g, tp, tt=64):
    t, d = h1.shape
    ts = t - tp
    tt = _tile_rows(math.gcd(tp, ts), tt)
    npt = tp // tt
    return pl.pallas_call(
        functools.partial(_combine_kernel, tt=tt, n_prompt_tiles=npt),
        grid_spec=pltpu.PrefetchScalarGridSpec(
            num_scalar_prefetch=1,
            grid=(t // tt,),
            in_specs=[
                pl.BlockSpec((tt, d), lambda i, pos: (i, 0)),
                pl.BlockSpec(memory_space=pl.ANY),
                pl.BlockSpec((1, d), lambda i, pos: (0, 0)),
            ],
            out_specs=[
                pl.BlockSpec((tt, d), lambda i, pos: (jnp.minimum(i, npt - 1), 0)),
                pl.BlockSpec((tt, d), lambda i, pos: (jnp.maximum(i - npt, 0), 0)),
            ],
            scratch_shapes=[pltpu.VMEM((2, TOP_K, tt, d), F32), pltpu.SemaphoreType.DMA((2,))],
        ),
        out_shape=[jax.ShapeDtypeStruct((tp, d), F32), jax.ShapeDtypeStruct((ts, d), F32)],
        compiler_params=_params(("arbitrary",)),
        name="moe_combine",
    )(pos, h1, y, g.reshape(1, d))


def _rope_tables(pos):
    half = (HEAD_W // 2) // 8
    inv_freq = ROPE_THETA ** (-jnp.arange(half, dtype=F32) * (2.0 / (2 * half)))
    ang = pos.astype(F32)[:, None] * inv_freq[None, :]
    cos, sin = jnp.cos(ang), jnp.sin(ang)
    ones = jnp.ones((pos.shape[0], HEAD_W // 2 - 2 * half), F32)
    c64 = jnp.concatenate([cos, cos, ones], axis=1)
    s64 = jnp.concatenate([-sin, sin, 0.0 * ones], axis=1)
    return jnp.concatenate([c64, c64], axis=1), jnp.concatenate([s64, s64], axis=1)


def _routing(logits, n_experts, n_blocks):
    t = logits.shape[0]
    top_v, top_i = lax.top_k(logits, TOP_K)
    gate_w = jax.nn.softmax(top_v, axis=-1)
    flat_e = top_i.reshape(-1).astype(jnp.int32)
    flat_t = jnp.repeat(jnp.arange(t, dtype=jnp.int32), TOP_K)
    flat_w = gate_w.reshape(-1)
    order = jnp.argsort(flat_e).astype(jnp.int32)
    slot_of = jnp.argsort(order).astype(jnp.int32)
    experts = jnp.arange(n_experts, dtype=jnp.int32)
    counts = jnp.sum((flat_e[:, None] == experts[None, :]).astype(jnp.int32), axis=0)
    starts = jnp.cumsum(counts) - counts
    padded = (counts + MOE_BLOCK - 1) // MOE_BLOCK * MOE_BLOCK
    padded_end = jnp.cumsum(padded)
    padded_start = padded_end - padded
    n_rows = n_blocks * MOE_BLOCK
    rows = jnp.arange(n_rows, dtype=jnp.int32)
    row_e = jnp.minimum(jnp.sum((rows[:, None] >= padded_end[None, :]).astype(jnp.int32), axis=1),
                        n_experts - 1)
    offset = rows - padded_start[row_e]
    valid = offset < counts[row_e]
    assign = order[jnp.clip(starts[row_e] + offset, 0, t * TOP_K - 1)]
    row_tok = jnp.where(valid, flat_t[assign], 0)
    row_w = jnp.where(valid, flat_w[assign], 0.0)
    pos = padded_start[flat_e] + slot_of - starts[flat_e]
    n_used = (padded_end[-1] // MOE_BLOCK).astype(jnp.int32)
    blocks = padded // MOE_BLOCK
    per_expert = (blocks + CHUNK_BLOCKS - 1) // CHUNK_BLOCKS
    chunk_end = jnp.cumsum(per_expert)
    n_chunks = chunk_end[-1]
    n_slots = -(-n_blocks // CHUNK_BLOCKS) + n_experts
    slots = jnp.arange(n_slots, dtype=jnp.int32)
    s_eff = jnp.minimum(slots, n_chunks - 1)
    chunk_e = jnp.sum((s_eff[:, None] >= chunk_end[None, :]).astype(jnp.int32), axis=1)
    k = s_eff - (chunk_end - per_expert)[chunk_e]
    real = slots < n_chunks
    chunk_r0 = jnp.where(real, padded_start[chunk_e] // MOE_BLOCK + k * CHUNK_BLOCKS, 0)
    chunk_n = jnp.where(real, jnp.clip(blocks[chunk_e] - k * CHUNK_BLOCKS, 0, CHUNK_BLOCKS), 0)
    meta = jnp.stack([n_chunks, n_used]).astype(jnp.int32)
    chunks = (chunk_e.astype(jnp.int32), chunk_r0.astype(jnp.int32), chunk_n.astype(jnp.int32), meta)
    return row_tok, row_w, pos, chunks, n_used.reshape(1)


def kernel(x_prompt, x_sample, cache_fox_k, cache_fox_v, cache_fox_logf, cache_diff_k, cache_diff_v, norm_mix, w_in, b_forget, lambda_q1, lambda_k1, lambda_q2, lambda_k2, diff_subln, w_branch, w_out, norm_ffn, w_router, b_router, moe_w_gate, moe_b_gate, moe_w_up, moe_b_up, moe_w_down, moe_b_down, norm_final):
    batch, seq, d = x_prompt.shape
    dbatch, dseq, _ = x_sample.shape
    depth, _, past, n_heads, _ = cache_fox_k.shape
    assert depth == 1 and cache_fox_k.shape[-1] == HEAD_W and cache_diff_k.shape[-1] == HEAD_W
    assert dseq == CHUNK and past % CHUNK == 0, "sample queries must form exactly the newest chunk"
    mix = n_heads * HEAD_W
    n_experts = w_router.shape[-1]
    tp, ts = batch * seq, dbatch * dseq
    t = tp + ts
    assert (t * TOP_K) % MOE_BLOCK == 0
    xp = x_prompt.reshape(tp, d)
    xs = x_sample.reshape(ts, d)

    off_f = 3 * mix
    off_dq = off_f + n_heads
    off_dv = off_dq + 2 * mix
    off_gate = off_dv + mix
    w = w_in[0]
    w_seg = lambda col0: w[:, col0:col0 + mix].astype(BF16)
    w_fgt = jnp.pad(w[:, off_f:off_dq], ((0, 0), (0, LANE - n_heads))).astype(BF16)
    b_fgt = jnp.pad(b_forget[0], (0, LANE - n_heads)).reshape(1, LANE)
    w_gates = w[:, off_gate:].astype(BF16)

    row_tile = _tile(math.gcd(tp, ts), 1024)
    xn = _norm_pair(xp, xs, norm_mix[0], _tile(math.gcd(tp, ts), 256))

    pos_all = jnp.concatenate([jnp.tile(jnp.arange(seq), batch), jnp.tile(past + jnp.arange(dseq), dbatch)])
    cos_t, sin_t = _rope_tables(pos_all)

    seg = lambda col0, mode, extra, name: _proj(xn, w_seg(col0), mix, mode, extra=extra, tm=row_tile,
                                                name=name, split=tp)
    fq_p, fq_s = seg(0, "plain", (), "proj_fox_q")
    fk_p, fk_s = seg(mix, "plain", (), "proj_fox_k")
    fv_p, fv_s = seg(2 * mix, "plain", (), "proj_fox_v")
    dq_p, dq_s = seg(off_dq, "rope", (cos_t, sin_t), "proj_diff_q")
    dk_p, dk_s = seg(off_dq + mix, "rope", (cos_t, sin_t), "proj_diff_k")
    dv_p, dv_s = seg(off_dv, "plain", (), "proj_diff_v")
    logf = _proj(xn, w_fgt, LANE, "logsig", extra=(b_fgt,), tm=row_tile, name="proj_forget")[:, :n_heads]
    gates = _proj(xn, w_gates, 2 * d, "sigmoid", tm=row_tile, name="proj_gates")

    logf_p = logf[:tp].reshape(batch, seq, n_heads)
    logf_s = logf[tp:].reshape(dbatch, dseq, n_heads)
    ck_p_flat = _cumsum_seq(logf_p)
    ck_s = _cumsum_seq(jnp.concatenate([cache_fox_logf[0].astype(F32), logf_s], axis=1))
    cum_p = ck_p_flat.transpose(0, 2, 1)
    tq = _tile(seq, 512)
    ck_p = ck_p_flat.reshape(batch, n_heads, seq // tq, tq).transpose(0, 2, 1, 3)
    cq_s = ck_s[:, :, past:].transpose(0, 2, 1).reshape(ts, n_heads)

    f32 = F32
    lam_init = 0.8 - 0.6 * math.exp(-0.3 * 0)
    lam = (jnp.exp(jnp.sum(lambda_q1[0].astype(f32) * lambda_k1[0].astype(f32)))
           - jnp.exp(jnp.sum(lambda_q2[0].astype(f32) * lambda_k2[0].astype(f32))) + lam_init).reshape(1)
    subln = diff_subln[0].reshape(1, HEAD_W)
    nh = n_heads
    fox_scale = HEAD_W ** -0.5
    diff_scale = (HEAD_W // 2) ** -0.5

    o_fox_p = _prompt_attention("fox", fq_p, 0, fk_p, 0, fv_p, 0, batch, seq, nh,
                                (cum_p.reshape(tp, nh), ck_p), fox_scale)
    o_diff_p = _prompt_attention("diff", dq_p, 0, dk_p, 0, dv_p, 0, batch, seq, nh,
                                 (lam, subln), diff_scale, post_scale=1.0 - lam_init)
    o_fox_s = _sample_attention("fox", fq_s, 0, cache_fox_k[0], cache_fox_v[0], fk_s, 0, fv_s, 0,
                                0, dbatch, dseq, past, nh,
                                (cq_s, ck_s[:, :, :past], ck_s[:, :, past:]), fox_scale)
    o_diff_s = _sample_attention("diff", dq_s, 0, cache_diff_k[0], cache_diff_v[0], dk_s, 0, dv_s, 0,
                                 0, dbatch, dseq, past, nh, (lam, subln), diff_scale,
                                 post_scale=1.0 - lam_init)

    merged = _merge(o_fox_p, o_fox_s, o_diff_p, o_diff_s, w_branch[0, 0].astype(BF16),
                    w_branch[0, 1].astype(BF16), gates, tm=row_tile)
    h1 = _out_proj(merged, w_out[0].astype(BF16), xp, xs, tm=row_tile)

    w_r = jnp.pad(w_router[0], ((0, 0), (0, LANE - n_experts)))
    b_r = jnp.pad(b_router[0].astype(F32), (0, LANE - n_experts)).reshape(1, LANE)
    logits, hn_packed = _router(h1, norm_ffn[0], w_r, b_r)
    logits = logits[:, :n_experts]
    n_blocks = (t * TOP_K) // MOE_BLOCK + n_experts
    row_tok, row_w, pos, chunks, n_used = _routing(logits, n_experts, n_blocks)
    xs_sorted = _gather_rows(hn_packed, row_tok, n_used, n_blocks)
    hmid = _expert_matmul("up", xs_sorted, [moe_w_gate[0], moe_w_up[0]], [moe_b_gate[0], moe_b_up[0]],
                          chunks, tn=256)
    row_w_b = jnp.broadcast_to(row_w[:, None], (row_w.shape[0], LANE))
    y_rows = _expert_matmul("down", hmid, [moe_w_down[0]], [moe_b_down[0]], chunks, row_w=row_w_b,
                            tn=512)
    y_p, y_s = _combine(h1, y_rows, pos, norm_final, tp)

    def state(arr, b, s):
        return arr.reshape(1, b, s, n_heads, HEAD_W)

    outs_p = (state(fk_p, batch, seq), state(fv_p, batch, seq), logf_p[None],
              state(dk_p, batch, seq), state(dv_p, batch, seq))
    outs_s = (state(fk_s, dbatch, dseq), state(fv_s, dbatch, dseq), logf_s[None],
              state(dk_s, dbatch, dseq), state(dv_s, dbatch, dseq))
    return (y_p.reshape(batch, seq, d), y_s.reshape(dbatch, dseq, d)) + outs_p + outs_s
```
